```python
import math
import jax, jax.numpy as jnp
from jax import lax
import numpy as np

D_MODEL = 1024
BATCH = 2
SEQ = 16384
DEPTH = 4

PLE_DIM = 256
D_FF = 2816
NORM_EPS = 1e-5
DN_ALPHA = (2 * DEPTH) ** 0.25
DN_BETA = (8 * DEPTH) ** -0.25
GDN_HEADS = 4
GDN_DK = 128
GDN_DV = 128
GDN_CONV = 4
GDN_CHUNK = 64
GDN_QK = GDN_HEADS * GDN_DK
GDN_VW = GDN_HEADS * GDN_DV
SB_HEADS = 4
SB_DH = 128
SB_BLOCK = 128
SB_W = SB_HEADS * SB_DH
RET_HEADS = 4
RET_DK = 128
RET_DV = 256
RET_CHUNK = 128
RET_QK = RET_HEADS * RET_DK
RET_VW = RET_HEADS * RET_DV
ROPE_BASE = 10000.0
IN_SIZES = (GDN_QK, GDN_QK, GDN_VW, GDN_VW, GDN_HEADS, GDN_HEADS,
            SB_W, SB_W, SB_W,
            RET_QK, RET_QK, RET_VW, RET_VW,
            D_MODEL, D_MODEL, D_MODEL)
D_IN = sum(IN_SIZES)
GDN_CONV_CH = 2 * GDN_QK + GDN_VW

kernel_name = 'hybrid_gdn_stickbreak_retention_deepnorm'


def _split_points():
    pts, acc = [], 0
    for s in IN_SIZES[:-1]:
        acc += s
        pts.append(acc)
    return pts


def layer_norm(x, g, b):
    xf = x.astype(jnp.float32)
    mu = xf.mean(-1, keepdims=True)
    var = jnp.square(xf - mu).mean(-1, keepdims=True)
    return ((xf - mu) * lax.rsqrt(var + NORM_EPS) * g.astype(jnp.float32) + b.astype(jnp.float32)).astype(x.dtype)


def rms_norm(t, w):
    return t * lax.rsqrt(jnp.mean(jnp.square(t), -1, keepdims=True) + NORM_EPS) * w.astype(jnp.float32)


def group_norm(t, w):
    mu = t.mean(-1, keepdims=True)
    var = jnp.square(t - mu).mean(-1, keepdims=True)
    return (t - mu) * lax.rsqrt(var + NORM_EPS) * w.astype(jnp.float32)


def l2norm(t):
    return t * lax.rsqrt(jnp.sum(jnp.square(t), -1, keepdims=True) + 1e-6)


def swiglu(x, w13, w2):
    g, u = jnp.split(x @ w13, 2, axis=-1)
    return (jax.nn.silu(g) * u) @ w2


def to_heads(t, n_heads):
    b, s, _ = t.shape
    return t.reshape(b, s, n_heads, -1).transpose(0, 2, 1, 3)


def from_heads(t):
    b, h, s, d = t.shape
    return t.transpose(0, 2, 1, 3).reshape(b, s, h * d)


def causal_depthwise_conv(x, w):
    width, ch = w.shape
    return lax.conv_general_dilated(x, w[:, None, :], window_strides=(1,), padding=[(width - 1, 0)],
                                    dimension_numbers=('NWC', 'WIO', 'NWC'), feature_group_count=ch)


def rotary(t, pos):
    half = t.shape[-1] // 2
    inv = ROPE_BASE ** (-jnp.arange(half, dtype=jnp.float32) / half)
    ang = pos.astype(jnp.float32)[:, None] * inv[None, :]
    cos, sin = jnp.cos(ang), jnp.sin(ang)
    t1, t2 = t[..., :half], t[..., half:]
    return jnp.concatenate([t1 * cos - t2 * sin, t1 * sin + t2 * cos], axis=-1)


def gated_delta_rule(q, k, v, beta, g):
    b, h, s, dk = q.shape
    dv = v.shape[-1]
    c = GDN_CHUNK
    n = s // c
    f32 = jnp.float32
    q, k, v = (t.reshape(b, h, n, c, -1) for t in (q, k, v))
    beta = beta.reshape(b, h, n, c, 1)
    G = jnp.cumsum(g.reshape(b, h, n, c), axis=-1)
    idx = jnp.arange(c)
    incl = idx[:, None] >= idx[None, :]
    strict = idx[:, None] > idx[None, :]
    decay = jnp.exp(jnp.where(incl, G[..., :, None] - G[..., None, :], -jnp.inf))
    kb = k * beta
    L = jnp.where(strict, jnp.einsum('bhnck,bhnsk->bhncs', kb, k) * decay, 0.0)
    rhs = jnp.concatenate([v * beta, kb * jnp.exp(G)[..., None]], axis=-1)
    sol = lax.linalg.triangular_solve(L + jnp.eye(c, dtype=f32), rhs, left_side=True, lower=True)
    u, w = sol[..., :dv], sol[..., dv:]
    a_qk = jnp.einsum('bhnck,bhnsk->bhncs', q, k) * decay
    q_dec = q * jnp.exp(G)[..., None]
    g_last = G[..., -1]
    k_dec = k * jnp.exp(g_last[..., None] - G)[..., None]

    def step(state, xs):
        u_c, w_c, q_c, k_c, a_c, gl = xs
        v_new = u_c - jnp.einsum('bhck,bhkv->bhcv', w_c, state)
        o = jnp.einsum('bhck,bhkv->bhcv', q_c, state) + jnp.einsum('bhcs,bhsv->bhcv', a_c, v_new)
        state = state * jnp.exp(gl)[..., None, None] + jnp.einsum('bhck,bhcv->bhkv', k_c, v_new)
        return state, o

    xs = tuple(jnp.moveaxis(t, 2, 0) for t in (u, w, q_dec, k_dec, a_qk, g_last))
    _, o = lax.scan(step, jnp.zeros((b, h, dk, dv), f32), xs)
    return jnp.moveaxis(o, 0, 2).reshape(b, h, s, dv)


def stick_breaking(q, k, v):
    b, h, s, d = q.shape
    dv = v.shape[-1]
    nb = s // SB_BLOCK
    scale = d ** -0.5
    offs = jnp.arange(SB_BLOCK)
    q_blocks = jnp.moveaxis(q.reshape(b, h, nb, SB_BLOCK, d), 2, 0)

    def one_block(args):
        qi, q_blk = args
        tq = qi * SB_BLOCK + offs

        def body(it, carry):
            acc, out = carry
            kj = qi - it
            k_blk = lax.dynamic_slice_in_dim(k, kj * SB_BLOCK, SB_BLOCK, axis=2)
            v_blk = lax.dynamic_slice_in_dim(v, kj * SB_BLOCK, SB_BLOCK, axis=2)
            z = jnp.einsum('bhtd,bhsd->bhts', q_blk, k_blk) * scale
            valid = (kj * SB_BLOCK + offs)[None, :] < tq[:, None]
            log_beta = jax.nn.log_sigmoid(z)
            log_keep = jnp.where(valid, jax.nn.log_sigmoid(-z), 0.0)
            cs = jnp.cumsum(log_keep, axis=-1)
            tot = cs[..., -1:]
            log_w = log_beta + (tot - cs) + acc[..., None]
            wgt = jnp.where(valid, jnp.exp(log_w), 0.0)
            out = out + jnp.einsum('bhts,bhsv->bhtv', wgt, v_blk)
            return acc + tot[..., 0], out

        init = (jnp.zeros((b, h, SB_BLOCK), jnp.float32), jnp.zeros((b, h, SB_BLOCK, dv), jnp.float32))
        _, out = lax.fori_loop(0, qi + 1, body, init)
        return out

    outs = lax.map(one_block, (jnp.arange(nb), q_blocks))
    return jnp.moveaxis(outs, 0, 2).reshape(b, h, s, dv)


def retention(q, k, v):
    b, h, s, dk = q.shape
    dv = v.shape[-1]
    c = RET_CHUNK
    n = s // c
    lg = jnp.log(1.0 - jnp.exp2(-5.0 - jnp.arange(h, dtype=jnp.float32)))[:, None]
    idx = jnp.arange(c, dtype=jnp.float32)
    rel = idx[:, None] - idx[None, :]
    dmat = jnp.where(rel >= 0, jnp.exp(jnp.maximum(rel, 0.0) * lg[:, :, None]), 0.0)
    q_dec = jnp.exp((idx + 1.0) * lg)[:, :, None]
    k_dec = jnp.exp((c - 1.0 - idx) * lg)[:, :, None]
    c_dec = jnp.exp(c * lg)[:, :, None]

    def step(state, xs):
        q_c, k_c, v_c = xs
        a = jnp.einsum('bhck,bhsk->bhcs', q_c, k_c) * dmat
        o = jnp.einsum('bhcs,bhsv->bhcv', a, v_c) + jnp.einsum('bhck,bhkv->bhcv', q_c * q_dec, state)
        state = state * c_dec + jnp.einsum('bhck,bhcv->bhkv', k_c * k_dec, v_c)
        return state, o

    xs = tuple(jnp.moveaxis(t.reshape(b, h, n, c, -1), 2, 0) for t in (q, k, v))
    _, o = lax.scan(step, jnp.zeros((b, h, dk, dv), jnp.float32), xs)
    return jnp.moveaxis(o, 0, 2).reshape(b, h, s, dv)


def hybrid_mixer(h, w_in, gdn_conv_w, gdn_a_log, gdn_dt_bias, gdn_norm_w, ret_norm_w,
                 w_br_gdn, w_br_sb, w_br_ret, w_out):
    dt = h.dtype
    f32 = jnp.float32
    s = h.shape[1]
    (gq, gk, gv, gz, gb, ga, sq, sk, sv, rq, rk, rv, rg, ma, mb, mc) = jnp.split(h @ w_in, _split_points(), axis=-1)

    qkv = jax.nn.silu(causal_depthwise_conv(jnp.concatenate([gq, gk, gv], axis=-1), gdn_conv_w)).astype(f32)
    aq, ak, av = jnp.split(qkv, [GDN_QK, 2 * GDN_QK], axis=-1)
    aq = l2norm(to_heads(aq, GDN_HEADS)) * GDN_DK ** -0.5
    ak = l2norm(to_heads(ak, GDN_HEADS))
    av = to_heads(av, GDN_HEADS)
    beta = jax.nn.sigmoid(gb.astype(f32)).transpose(0, 2, 1)
    log_decay = (-jnp.exp(gdn_a_log.astype(f32)) *
                 jax.nn.softplus(ga.astype(f32) + gdn_dt_bias.astype(f32))).transpose(0, 2, 1)
    oa = gated_delta_rule(aq, ak, av, beta, log_decay)
    oa = from_heads(rms_norm(oa, gdn_norm_w)) * jax.nn.silu(gz.astype(f32))
    ya = oa.astype(dt) @ w_br_gdn

    ob = stick_breaking(to_heads(sq.astype(f32), SB_HEADS), to_heads(sk.astype(f32), SB_HEADS),
                        to_heads(sv.astype(f32), SB_HEADS))
    yb = from_heads(ob).astype(dt) @ w_br_sb

    pos = jnp.arange(s)
    cq = rotary(to_heads(rq.astype(f32), RET_HEADS), pos)
    ck = rotary(to_heads(rk.astype(f32), RET_HEADS), pos) * RET_DK ** -0.5
    oc = retention(cq, ck, to_heads(rv.astype(f32), RET_HEADS))
    oc = from_heads(group_norm(oc, ret_norm_w)) * jax.nn.silu(rg.astype(f32))
    yc = oc.astype(dt) @ w_br_ret

    merged = jax.nn.sigmoid(ma) * ya + jax.nn.sigmoid(mb) * yb + jax.nn.sigmoid(mc) * yc
    return merged @ w_out


def setup_inputs(seed: int = 0) -> dict:
    key = jax.random.key(seed)
    ks = jax.random.split(key, 24)
    f32 = jnp.float32
    nrm = lambda k, shape, scale: jax.random.normal(k, shape, f32) * scale
    x = jax.random.normal(ks[0], (BATCH, SEQ, D_MODEL), f32)
    p = jax.random.normal(ks[1], (DEPTH, BATCH, SEQ, PLE_DIM), f32)
    ffn1_w13 = nrm(ks[2], (DEPTH, D_MODEL, 2 * D_FF), D_MODEL ** -0.5)
    ffn1_w2 = nrm(ks[3], (DEPTH, D_FF, D_MODEL), D_FF ** -0.5 * DN_BETA)
    w_in = nrm(ks[4], (DEPTH, D_MODEL, D_IN), D_MODEL ** -0.5)
    gdn_conv_w = nrm(ks[5], (DEPTH, GDN_CONV, GDN_CONV_CH), GDN_CONV ** -0.5)
    gdn_a_log = jnp.log(jax.random.uniform(ks[6], (DEPTH, GDN_HEADS), f32, 1.0, 16.0))
    dt0 = jnp.exp(jax.random.uniform(ks[7], (DEPTH, GDN_HEADS), f32, math.log(1e-3), math.log(1e-1)))
    gdn_dt_bias = dt0 + jnp.log(-jnp.expm1(-dt0))
    gdn_norm_w = 1.0 + nrm(ks[8], (DEPTH, GDN_DV), 0.02)
    ret_norm_w = 1.0 + nrm(ks[9], (DEPTH, RET_DV), 0.02)
    w_br_gdn = nrm(ks[10], (DEPTH, GDN_VW, D_MODEL), GDN_VW ** -0.5)
    w_br_sb = nrm(ks[11], (DEPTH, SB_W, D_MODEL), SB_W ** -0.5)
    w_br_ret = nrm(ks[12], (DEPTH, RET_VW, D_MODEL), RET_VW ** -0.5)
    w_out = nrm(ks[13], (DEPTH, D_MODEL, D_MODEL), D_MODEL ** -0.5 * DN_BETA)
    ffn2_w13 = nrm(ks[14], (DEPTH, D_MODEL, 2 * D_FF), D_MODEL ** -0.5)
    ffn2_w2 = nrm(ks[15], (DEPTH, D_FF, D_MODEL), D_FF ** -0.5 * DN_BETA)
    ln_g = 1.0 + nrm(ks[16], (DEPTH, 3, D_MODEL), 0.02)
    ln_b = nrm(ks[17], (DEPTH, 3, D_MODEL), 0.02)
    w_ple_gate = nrm(ks[18], (DEPTH, D_MODEL, D_MODEL), D_MODEL ** -0.5)
    w_ple_proj = nrm(ks[19], (DEPTH, PLE_DIM, D_MODEL), PLE_DIM ** -0.5 * DN_BETA)
    return {'x': x, 'p': p, 'ffn1_w13': ffn1_w13, 'ffn1_w2': ffn1_w2, 'w_in': w_in,
            'gdn_conv_w': gdn_conv_w, 'gdn_a_log': gdn_a_log, 'gdn_dt_bias': gdn_dt_bias,
            'gdn_norm_w': gdn_norm_w, 'ret_norm_w': ret_norm_w, 'w_br_gdn': w_br_gdn,
            'w_br_sb': w_br_sb, 'w_br_ret': w_br_ret, 'w_out': w_out, 'ffn2_w13': ffn2_w13,
            'ffn2_w2': ffn2_w2, 'ln_g': ln_g, 'ln_b': ln_b, 'w_ple_gate': w_ple_gate,
            'w_ple_proj': w_ple_proj}


def reference(x, p, ffn1_w13, ffn1_w2, w_in, gdn_conv_w, gdn_a_log, gdn_dt_bias, gdn_norm_w,
              ret_norm_w, w_br_gdn, w_br_sb, w_br_ret, w_out, ffn2_w13, ffn2_w2, ln_g, ln_b,
              w_ple_gate, w_ple_proj):
    for i in range(DEPTH):
        x = layer_norm(DN_ALPHA * x + 0.5 * swiglu(x, ffn1_w13[i], ffn1_w2[i]), ln_g[i, 0], ln_b[i, 0])
        mix = hybrid_mixer(x, w_in[i], gdn_conv_w[i], gdn_a_log[i], gdn_dt_bias[i], gdn_norm_w[i],
                           ret_norm_w[i], w_br_gdn[i], w_br_sb[i], w_br_ret[i], w_out[i])
        x = layer_norm(DN_ALPHA * x + mix, ln_g[i, 1], ln_b[i, 1])
        x = layer_norm(DN_ALPHA * x + 0.5 * swiglu(x, ffn2_w13[i], ffn2_w2[i]), ln_g[i, 2], ln_b[i, 2])
        x = x + jax.nn.sigmoid(x @ w_ple_gate[i]) * (p[i] @ w_ple_proj[i])
    return x
```

```python
import functools
import math

import jax
import jax.numpy as jnp
from jax import lax
from jax.experimental import pallas as pl
from jax.experimental.pallas import tpu as pltpu

F32 = jnp.float32
BF16 = jnp.bfloat16

NORM_EPS = 1e-5
HEADS = 4
HEAD_DIM = 128
RET_DV = 256
GDN_CONV = 4
ROPE_BASE = 10000.0
CHUNK = 128
CONV_HALO = 8
SB_STICK_EXHAUSTED = 110.0
VMEM_LIMIT = 56 * 1024 * 1024


def _dot(a, b):
    return jnp.dot(a, b, preferred_element_type=F32)


def _dot_nt(a, b):
    return lax.dot_general(a, b, (((1,), (1,)), ((), ())), preferred_element_type=F32)


def _dot_tn(a, b):
    return lax.dot_general(a, b, (((0,), (0,)), ((), ())), preferred_element_type=F32)


def _split(a):
    hi = a.astype(BF16)
    lo = (a - hi.astype(F32)).astype(BF16)
    return hi, lo


def _dot3(a, b):
    ah, al = _split(a)
    bh, bl = _split(b)
    return _dot(ah, bh) + _dot(ah, bl) + _dot(al, bh)


def _softplus(x):
    return jnp.maximum(x, 0.0) + jnp.log(1.0 + jnp.exp(-jnp.abs(x)))


def _sigmoid(x):
    return 1.0 / (1.0 + jnp.exp(-x))


def _layer_norm(y, g, b):
    mu = jnp.mean(y, axis=-1, keepdims=True)
    d = y - mu
    var = jnp.mean(d * d, axis=-1, keepdims=True)
    return d * lax.rsqrt(var + NORM_EPS) * g + b


def _params(*sem):
    return pltpu.CompilerParams(dimension_semantics=sem, vmem_limit_bytes=VMEM_LIMIT)


def _ffn_kernel(alpha, with_ple, x_ref, w1_ref, w3_ref, w2_ref, g_ref, b_ref, *rest):
    if with_ple:
        p_ref, wpg_ref, wpe_ref, o_ref, ob_ref, xb_scr, acc_scr = rest
    else:
        o_ref, ob_ref, xb_scr, acc_scr = rest
    j = pl.program_id(1)

    @pl.when(j == 0)
    def _():
        xb_scr[...] = x_ref[...].astype(BF16)
        acc_scr[...] = jnp.zeros_like(acc_scr)

    xb = xb_scr[...]
    g = _dot(xb, w1_ref[...])
    u = _dot(xb, w3_ref[...])
    h = (g * _sigmoid(g) * u).astype(BF16)
    acc_scr[...] += _dot(h, w2_ref[...])

    @pl.when(j == pl.num_programs(1) - 1)
    def _():
        y = _layer_norm(alpha * x_ref[...] + 0.5 * acc_scr[...], g_ref[...], b_ref[...])
        if with_ple:
            gate = _sigmoid(_dot(y.astype(BF16), wpg_ref[...]))
            y = y + gate * _dot(p_ref[...].astype(BF16), wpe_ref[...])
        o_ref[...] = y
        ob_ref[...] = y.astype(BF16)


def _ffn(x, w1, w3, w2, g, b, alpha, ple=None, tm=512, tf=1408):
    t, d = x.shape
    f = w1.shape[1]
    tf = min(tf, f)
    tm = min(tm, t)
    in_specs = [
        pl.BlockSpec((tm, d), lambda i, j: (i, 0)),
        pl.BlockSpec((d, tf), lambda i, j: (0, j)),
        pl.BlockSpec((d, tf), lambda i, j: (0, j)),
        pl.BlockSpec((tf, d), lambda i, j: (j, 0)),
        pl.BlockSpec((1, d), lambda i, j: (0, 0)),
        pl.BlockSpec((1, d), lambda i, j: (0, 0)),
    ]
    args = [x, w1, w3, w2, g, b]
    if ple is not None:
        p, wpg, wpe = ple
        in_specs += [
            pl.BlockSpec((tm, p.shape[1]), lambda i, j: (i, 0)),
            pl.BlockSpec(wpg.shape, lambda i, j: (0, 0)),
            pl.BlockSpec(wpe.shape, lambda i, j: (0, 0)),
        ]
        args += [p, wpg, wpe]
    return pl.pallas_call(
        functools.partial(_ffn_kernel, alpha, ple is not None),
        grid=(t // tm, f // tf),
        in_specs=in_specs,
        out_specs=[pl.BlockSpec((tm, d), lambda i, j: (i, 0)),
                   pl.BlockSpec((tm, d), lambda i, j: (i, 0))],
        out_shape=[jax.ShapeDtypeStruct((t, d), F32), jax.ShapeDtypeStruct((t, d), BF16)],
        scratch_shapes=[pltpu.VMEM((tm, d), BF16), pltpu.VMEM((tm, d), F32)],
        compiler_params=_params("parallel", "arbitrary"),
        name="ffn_ln_ple" if ple is not None else "ffn_ln",
    )(*args)


def _proj_kernel(hb_ref, wg_ref, ws_ref, wrqk_ref, wrv_ref, xg_ref, sb_ref, rqk_ref, rv_ref):
    hb = hb_ref[...]
    xg_ref[...] = _dot(hb, wg_ref[...])
    sb_ref[...] = _dot(hb, ws_ref[...]).astype(BF16)
    rqk_ref[...] = _dot(hb, wrqk_ref[...])
    rv_ref[...] = _dot(hb, wrv_ref[...]).astype(BF16)


def _proj(hb, wg, ws, wrqk, wrv, tm=512):
    t, d = hb.shape
    tm = min(tm, t)
    ws_ = [wg, ws, wrqk, wrv]
    dts = [F32, BF16, F32, BF16]
    return pl.pallas_call(
        _proj_kernel,
        grid=(t // tm,),
        in_specs=[pl.BlockSpec((tm, d), lambda i: (i, 0))]
        + [pl.BlockSpec(w.shape, lambda i: (0, 0)) for w in ws_],
        out_specs=[pl.BlockSpec((tm, w.shape[1]), lambda i: (i, 0)) for w in ws_],
        out_shape=[jax.ShapeDtypeStruct((t, w.shape[1]), dt) for w, dt in zip(ws_, dts)],
        compiler_params=_params("parallel"),
        name="mixer_in_proj",
    )(hb, *ws_)


def _unit_lower_inverse(low, ii, jj):
    eye = (ii == jj).astype(F32)
    x = jnp.where((ii >> 3) == (jj >> 3), -low, 0.0)
    x2 = _dot3(x, x)
    x4 = _dot3(x2, x2)
    t = eye + x
    t = t + _dot3(t, x2)
    t = t + _dot3(t, x4)
    shift = 3
    while (1 << shift) < CHUNK:
        off = jnp.where(((ii >> (shift + 1)) == (jj >> (shift + 1))) & ((ii >> shift) != (jj >> shift)), low, 0.0)
        t = t - _dot3(t, _dot3(off, t))
        shift += 1
    return t


def _gdn_kernel(hb_ref, xg_ref, cw_ref, wb_ref, wa_ref, wat_ref, alog_r_ref, dtb_r_ref,
                alog_c_ref, dtb_c_ref, nw_ref, o_ref, xpad_scr, qkv_scr, state_scr):
    tb = xg_ref.shape[1]
    width = HEADS * HEAD_DIM

    @pl.when(pl.program_id(1) == 0)
    def _():
        xpad_scr[0:CONV_HALO, :] = jnp.zeros((CONV_HALO, xpad_scr.shape[1]), F32)
        state_scr[...] = jnp.zeros_like(state_scr)

    xpad_scr[CONV_HALO:CONV_HALO + tb, :] = xg_ref[0]
    acc = cw_ref[GDN_CONV - 1:GDN_CONV, :] * xpad_scr[CONV_HALO:CONV_HALO + tb, :]
    for k in range(GDN_CONV - 1):
        off = CONV_HALO - (GDN_CONV - 1) + k
        acc = acc + cw_ref[k:k + 1, :] * xpad_scr[off:off + tb, :]
    xpad_scr[0:CONV_HALO, :] = xpad_scr[tb:tb + CONV_HALO, :]
    qkv_scr[...] = acc * _sigmoid(acc)

    ii = lax.broadcasted_iota(jnp.int32, (CHUNK, CHUNK), 0)
    jj = lax.broadcasted_iota(jnp.int32, (CHUNK, CHUNK), 1)
    tri_incl = (ii >= jj).astype(BF16)
    tri_incl_t = (ii <= jj).astype(BF16)
    neg_a_r = -jnp.exp(alog_r_ref[...])
    neg_a_c = -jnp.exp(alog_c_ref[...])
    nw = nw_ref[...]

    def l2n(t):
        return t * lax.rsqrt(jnp.sum(t * t, axis=-1, keepdims=True) + 1e-6)

    def chunk(c, carry):
        r0 = pl.multiple_of(c * CHUNK, CHUNK)
        hb_c = hb_ref[0, pl.ds(r0, CHUNK), :]
        beta_col = _sigmoid(_dot(hb_c, wb_ref[...]))
        g_col = neg_a_r * _softplus(_dot(hb_c, wa_ref[...]) + dtb_r_ref[...])
        g_row = neg_a_c * _softplus(_dot_nt(wat_ref[...], hb_c) + dtb_c_ref[...])
        for h in range(HEADS):
            lo = h * HEAD_DIM
            q = l2n(qkv_scr[pl.ds(r0, CHUNK), lo:lo + HEAD_DIM]) * (HEAD_DIM ** -0.5)
            k = l2n(qkv_scr[pl.ds(r0, CHUNK), width + lo:width + lo + HEAD_DIM])
            v = qkv_scr[pl.ds(r0, CHUNK), 2 * width + lo:2 * width + lo + HEAD_DIM]
            beta = jnp.broadcast_to(beta_col[:, h:h + 1], (CHUNK, CHUNK))
            gch, gcl = _split(jnp.broadcast_to(g_col[:, h:h + 1], (CHUNK, CHUNK)))
            grh, grl = _split(jnp.broadcast_to(g_row[h:h + 1, :], (CHUNK, CHUNK)))
            big_g_c = _dot(tri_incl, gch) + _dot(tri_incl, gcl)
            big_g_r = _dot(grh, tri_incl_t) + _dot(grl, tri_incl_t)
            dec = jnp.exp(jnp.where(ii >= jj, big_g_c - big_g_r, -1e30))
            kb = k * beta
            kbf = k.astype(BF16)
            low = jnp.where(ii > jj, _dot_nt(kb.astype(BF16), kbf) * dec, 0.0)
            t_inv = _unit_lower_inverse(low, ii, jj)
            e_g = jnp.exp(big_g_c)
            rhs = jnp.concatenate([v * beta, kb * e_g], axis=1).astype(BF16)
            sol = _dot(t_inv.astype(BF16), rhs)
            u = sol[:, :HEAD_DIM]
            w = sol[:, HEAD_DIM:]
            a_qk = _dot_nt(q.astype(BF16), kbf) * dec
            g_last = big_g_c[CHUNK - 1:CHUNK, :]
            k_dec = k * jnp.exp(g_last - big_g_c)
            state = state_scr[h]
            sbf = state.astype(BF16)
            v_new = u - _dot(w.astype(BF16), sbf)
            vnb = v_new.astype(BF16)
            o = _dot((q * e_g).astype(BF16), sbf) + _dot(a_qk.astype(BF16), vnb)
            state_scr[h] = state * jnp.exp(g_last) + _dot_tn(k_dec.astype(BF16), vnb)
            o = o * lax.rsqrt(jnp.mean(o * o, axis=-1, keepdims=True) + NORM_EPS) * nw
            o_ref[0, pl.ds(r0, CHUNK), lo:lo + HEAD_DIM] = o
        return carry

    lax.fori_loop(0, tb // CHUNK, chunk, 0)


def _gdn(hb, xg, conv_w, wb, wa, a_log, dt_bias, norm_w, tb=512):
    bsz, s, d = hb.shape
    tb = min(tb, s)
    cw = xg.shape[2]
    full = lambda a: pl.BlockSpec(a.shape, lambda b, i: (0,) * a.ndim)
    wat = wa.T
    smalls = [conv_w, wb, wa, wat, a_log.reshape(1, HEADS), dt_bias.reshape(1, HEADS),
              a_log.reshape(HEADS, 1), dt_bias.reshape(HEADS, 1), norm_w.reshape(1, HEAD_DIM)]
    return pl.pallas_call(
        _gdn_kernel,
        grid=(bsz, s // tb),
        in_specs=[pl.BlockSpec((1, tb, d), lambda b, i: (b, i, 0)),
                  pl.BlockSpec((1, tb, cw), lambda b, i: (b, i, 0))] + [full(a) for a in smalls],
        out_specs=pl.BlockSpec((1, tb, HEADS * HEAD_DIM), lambda b, i: (b, i, 0)),
        out_shape=jax.ShapeDtypeStruct((bsz, s, HEADS * HEAD_DIM), F32),
        scratch_shapes=[pltpu.VMEM((tb + CONV_HALO, cw), F32), pltpu.VMEM((tb, cw), F32),
                        pltpu.VMEM((HEADS, HEAD_DIM, HEAD_DIM), F32)],
        compiler_params=_params("parallel", "arbitrary"),
        name="gated_deltanet",
    )(hb, xg, *smalls)


def _sb_kernel(q_ref, k_ref, v_ref, o_ref):
    tq = q_ref.shape[1]
    tk = tq
    qi = pl.program_id(2)
    q = q_ref[0]
    row = lax.broadcasted_iota(jnp.int32, (tq, tk), 0)
    col = lax.broadcasted_iota(jnp.int32, (tq, tk), 1)
    suffix = jnp.concatenate([(row >= col).astype(BF16), jnp.ones((tk, tk), BF16)], axis=1)

    def cond(carry):
        kj, spent_min, _, _ = carry
        return jnp.logical_and(kj >= 0, spent_min < SB_STICK_EXHAUSTED)

    def body(carry):
        kj, _, spent, out = carry
        k0 = pl.multiple_of(kj * tk, tk)
        kb = k_ref[0, pl.ds(k0, tk), :]
        vb = v_ref[0, pl.ds(k0, tk), :]
        z = _dot_nt(q, kb)
        sp = _softplus(z)
        valid = (col + (kj - qi) * tk) < row
        nlk = jnp.where(valid, sp, 0.0)
        hi, lo = _split(nlk)
        sums = _dot(hi, suffix) + _dot(lo, suffix)
        later = sums[:, :tk] - nlk
        tot = sums[:, tk:]
        log_w = (z - sp) - later - spent
        wgt = jnp.where(valid, jnp.exp(log_w), 0.0)
        out = out + _dot(wgt.astype(BF16), vb)
        spent = spent + tot
        return kj - 1, jnp.min(spent), spent, out

    init = (qi, jnp.float32(0.0), jnp.zeros((tq, tk), F32), jnp.zeros((tq, v_ref.shape[2]), F32))
    _, _, _, out = lax.while_loop(cond, body, init)
    o_ref[0] = out.astype(o_ref.dtype)


def _stick_breaking(qkv, tq=128):
    bsz, s, _ = qkv.shape
    tq = min(tq, s)
    return pl.pallas_call(
        _sb_kernel,
        grid=(bsz, HEADS, s // tq),
        in_specs=[pl.BlockSpec((1, tq, HEAD_DIM), lambda b, h, i: (b, i, h)),
                  pl.BlockSpec((1, s, HEAD_DIM), lambda b, h, i: (b, 0, HEADS + h)),
                  pl.BlockSpec((1, s, HEAD_DIM), lambda b, h, i: (b, 0, 2 * HEADS + h))],
        out_specs=pl.BlockSpec((1, tq, HEAD_DIM), lambda b, h, i: (b, i, h)),
        out_shape=jax.ShapeDtypeStruct((bsz, s, HEADS * HEAD_DIM), BF16),
        compiler_params=_params("parallel", "parallel", "arbitrary"),
        name="stick_breaking",
    )(qkv, qkv, qkv)


def _ret_kernel(qk_ref, v_ref, cos_ref, sin_ref, nw_ref, o_ref, state_scr):
    tb = qk_ref.shape[1]
    width = HEADS * HEAD_DIM

    @pl.when(pl.program_id(1) == 0)
    def _():
        state_scr[...] = jnp.zeros_like(state_scr)

    ii = lax.broadcasted_iota(jnp.int32, (CHUNK, CHUNK), 0)
    jj = lax.broadcasted_iota(jnp.int32, (CHUNK, CHUNK), 1)
    rel = (ii - jj).astype(F32)
    pos = ii.astype(F32)
    nw = nw_ref[...]

    def rot(t, cos, sin):
        return t * cos + pltpu.roll(t, HEAD_DIM // 2, 1) * sin

    def chunk(c, carry):
        r0 = pl.multiple_of(c * CHUNK, CHUNK)
        cos = cos_ref[pl.ds(r0, CHUNK), :]
        sin = sin_ref[pl.ds(r0, CHUNK), :]
        for h in range(HEADS):
            lg = math.log(1.0 - 2.0 ** (-5.0 - h))
            lo = h * HEAD_DIM
            q = rot(qk_ref[0, pl.ds(r0, CHUNK), lo:lo + HEAD_DIM], cos, sin)
            k = rot(qk_ref[0, pl.ds(r0, CHUNK), width + lo:width + lo + HEAD_DIM], cos, sin) * (HEAD_DIM ** -0.5)
            v = v_ref[0, pl.ds(r0, CHUNK), h * RET_DV:(h + 1) * RET_DV]
            dmat = jnp.where(rel >= 0, jnp.exp(jnp.maximum(rel, 0.0) * lg), 0.0)
            a = _dot_nt(q.astype(BF16), k.astype(BF16)) * dmat
            q_dec = jnp.exp((pos + 1.0) * lg)
            k_dec = jnp.exp((CHUNK - 1.0 - pos) * lg)
            state = state_scr[h]
            o = _dot(a.astype(BF16), v) + _dot((q * q_dec).astype(BF16), state.astype(BF16))
            state_scr[h] = state * math.exp(CHUNK * lg) + _dot_tn((k * k_dec).astype(BF16), v)
            mu = jnp.mean(o, axis=-1, keepdims=True)
            dev = o - mu
            var = jnp.mean(dev * dev, axis=-1, keepdims=True)
            o_ref[0, pl.ds(r0, CHUNK), h * RET_DV:(h + 1) * RET_DV] = dev * lax.rsqrt(var + NORM_EPS) * nw
        return carry

    lax.fori_loop(0, tb // CHUNK, chunk, 0)


def _retention(rqk, rv, cos, sin, norm_w, tb=512):
    bsz, s, _ = rqk.shape
    tb = min(tb, s)
    return pl.pallas_call(
        _ret_kernel,
        grid=(bsz, s // tb),
        in_specs=[pl.BlockSpec((1, tb, 2 * HEADS * HEAD_DIM), lambda b, i: (b, i, 0)),
                  pl.BlockSpec((1, tb, HEADS * RET_DV), lambda b, i: (b, i, 0)),
                  pl.BlockSpec((tb, HEAD_DIM), lambda b, i: (i, 0)),
                  pl.BlockSpec((tb, HEAD_DIM), lambda b, i: (i, 0)),
                  pl.BlockSpec((1, RET_DV), lambda b, i: (0, 0))],
        out_specs=pl.BlockSpec((1, tb, HEADS * RET_DV), lambda b, i: (b, i, 0)),
        out_shape=jax.ShapeDtypeStruct((bsz, s, HEADS * RET_DV), F32),
        scratch_shapes=[pltpu.VMEM((HEADS, HEAD_DIM, RET_DV), F32)],
        compiler_params=_params("parallel", "arbitrary"),
        name="retention",
    )(rqk, rv, cos, sin, norm_w.reshape(1, RET_DV))


def _merge_kernel(alpha, x_ref, hb_ref, oa_ref, ob_ref, oc_ref, wgz_ref, wrg_ref, wma_ref, wmb_ref, wmc_ref,
                  wbg_ref, wbs_ref, wbr_ref, wo_ref, g_ref, b_ref, o_ref, obf_ref):
    hb = hb_ref[...]

    def silu(t):
        return t * _sigmoid(t)

    ya = _dot((oa_ref[...] * silu(_dot(hb, wgz_ref[...]))).astype(BF16), wbg_ref[...])
    merged = _sigmoid(_dot(hb, wma_ref[...])) * ya
    yb = _dot(ob_ref[...], wbs_ref[...])
    merged = merged + _sigmoid(_dot(hb, wmb_ref[...])) * yb
    yc = _dot((oc_ref[...] * silu(_dot(hb, wrg_ref[...]))).astype(BF16), wbr_ref[...])
    merged = merged + _sigmoid(_dot(hb, wmc_ref[...])) * yc
    mix = _dot(merged.astype(BF16), wo_ref[...])
    y = _layer_norm(alpha * x_ref[...] + mix, g_ref[...], b_ref[...])
    o_ref[...] = y
    obf_ref[...] = y.astype(BF16)


def _merge(x, hb, oa, ob, oc, weights, g, b, alpha, tm=256):
    t, d = x.shape
    tm = min(tm, t)
    acts = [x, hb, oa, ob, oc]
    consts = list(weights) + [g, b]
    return pl.pallas_call(
        functools.partial(_merge_kernel, alpha),
        grid=(t // tm,),
        in_specs=[pl.BlockSpec((tm, a.shape[1]), lambda i: (i, 0)) for a in acts]
        + [pl.BlockSpec(w.shape, lambda i: (0, 0)) for w in consts],
        out_specs=[pl.BlockSpec((tm, d), lambda i: (i, 0)), pl.BlockSpec((tm, d), lambda i: (i, 0))],
        out_shape=[jax.ShapeDtypeStruct((t, d), F32), jax.ShapeDtypeStruct((t, d), BF16)],
        compiler_params=_params("parallel"),
        name="mixer_merge_ln",
    )(*acts, *consts)


def _rotary_tables(s):
    half = HEAD_DIM // 2
    inv = ROPE_BASE ** (-jnp.arange(half, dtype=F32) / half)
    ang = jnp.arange(s).astype(F32)[:, None] * inv[None, :]
    cos, sin = jnp.cos(ang), jnp.sin(ang)
    return jnp.concatenate([cos, cos], axis=-1), jnp.concatenate([-sin, sin], axis=-1)


def kernel(x, p, ffn1_w13, ffn1_w2, w_in, gdn_conv_w, gdn_a_log, gdn_dt_bias, gdn_norm_w, ret_norm_w, w_br_gdn, w_br_sb, w_br_ret, w_out, ffn2_w13, ffn2_w2, ln_g, ln_b, w_ple_gate, w_ple_proj):
    bsz, s, d = x.shape
    depth = ffn1_w13.shape[0]
    t = bsz * s
    d_ff = ffn1_w2.shape[1]
    alpha = (2 * depth) ** 0.25
    hw = HEADS * HEAD_DIM
    rv_w = HEADS * RET_DV
    o_gq, o_gz, o_gb, o_ga = 0, 3 * hw, 4 * hw, 4 * hw + HEADS
    o_sq = 4 * hw + 2 * HEADS
    o_rq = o_sq + 3 * hw
    o_rv = o_rq + 2 * hw
    o_rg = o_rv + rv_w
    o_ma = o_rg + rv_w
    bf = lambda a: a.astype(BF16)
    cos, sin = _rotary_tables(s)

    xf = x.reshape(t, d)
    for i in range(depth):
        wi = w_in[i]
        row = lambda a: a.reshape(1, d)
        xf, xb = _ffn(xf, bf(ffn1_w13[i, :, :d_ff]), bf(ffn1_w13[i, :, d_ff:]), bf(ffn1_w2[i]),
                      row(ln_g[i, 0]), row(ln_b[i, 0]), alpha)
        w_sb = jnp.concatenate([wi[:, o_sq:o_sq + hw] * (HEAD_DIM ** -0.5), wi[:, o_sq + hw:o_sq + 3 * hw]], axis=1)
        xg, sqkv, rqk, rv = _proj(xb, bf(wi[:, o_gq:o_gq + 3 * hw]), bf(w_sb),
                                  bf(wi[:, o_rq:o_rq + 2 * hw]), bf(wi[:, o_rv:o_rv + rv_w]))
        hb3 = xb.reshape(bsz, s, d)
        oa = _gdn(hb3, xg.reshape(bsz, s, 3 * hw), gdn_conv_w[i], bf(wi[:, o_gb:o_gb + HEADS]),
                  bf(wi[:, o_ga:o_ga + HEADS]), gdn_a_log[i], gdn_dt_bias[i], gdn_norm_w[i])
        ob = _stick_breaking(sqkv.reshape(bsz, s, 3 * hw))
        oc = _retention(rqk.reshape(bsz, s, 2 * hw), rv.reshape(bsz, s, rv_w), cos, sin, ret_norm_w[i])
        weights = [bf(wi[:, o_gz:o_gz + hw]), bf(wi[:, o_rg:o_rg + rv_w]),
                   bf(wi[:, o_ma:o_ma + d]), bf(wi[:, o_ma + d:o_ma + 2 * d]), bf(wi[:, o_ma + 2 * d:o_ma + 3 * d]),
                   bf(w_br_gdn[i]), bf(w_br_sb[i]), bf(w_br_ret[i]), bf(w_out[i])]
        xf, xb = _merge(xf, xb, oa.reshape(t, hw), ob.reshape(t, hw), oc.reshape(t, rv_w), weights,
                        row(ln_g[i, 1]), row(ln_b[i, 1]), alpha)
        xf, xb = _ffn(xf, bf(ffn2_w13[i, :, :d_ff]), bf(ffn2_w13[i, :, d_ff:]), bf(ffn2_w2[i]),
                      row(ln_g[i, 2]), row(ln_b[i, 2]), alpha,
                      ple=(p[i].reshape(t, p.shape[-1]), bf(w_ple_gate[i]), bf(w_ple_proj[i])))
    return xf.reshape(bsz, s, d)
```

```python
import functools
import math

import jax
import jax.numpy as jnp
from jax import lax
from jax.experimental import pallas as pl
from jax.experimental.pallas import tpu as pltpu

F32 = jnp.float32
BF16 = jnp.bfloat16

NORM_EPS = 1e-5
HEADS = 4
HEAD_DIM = 128
RET_DV = 256
GDN_CONV = 4
ROPE_BASE = 10000.0
CHUNK = 128
CONV_HALO = 8
GDN_GROUP = 2
MERGE_CHUNK = 256
FFN_CHUNK = 256
PROJ_CHUNK = 256
SB_STICK_EXHAUSTED = 110.0
VMEM_LIMIT = 56 * 1024 * 1024


def _dot(a, b):
    return jnp.dot(a, b, preferred_element_type=F32)


def _dot_nt(a, b):
    return lax.dot_general(a, b, (((1,), (1,)), ((), ())), preferred_element_type=F32)


def _dot_tn(a, b):
    return lax.dot_general(a, b, (((0,), (0,)), ((), ())), preferred_element_type=F32)


def _split(a):
    hi = a.astype(BF16)
    lo = (a - hi.astype(F32)).astype(BF16)
    return hi, lo


def _dot3(a, b):
    ah, al = _split(a)
    bh, bl = _split(b)
    return _dot(ah, bh) + _dot(ah, bl) + _dot(al, bh)


def _softplus(x):
    return jnp.maximum(x, 0.0) + jnp.log(1.0 + jnp.exp(-jnp.abs(x)))


def _sigmoid(x):
    return 1.0 / (1.0 + jnp.exp(-x))


def _layer_norm(y, g, b):
    mu = jnp.mean(y, axis=-1, keepdims=True)
    d = y - mu
    var = jnp.mean(d * d, axis=-1, keepdims=True)
    return d * lax.rsqrt(var + NORM_EPS) * g + b


def _params(*sem):
    return pltpu.CompilerParams(dimension_semantics=sem, vmem_limit_bytes=VMEM_LIMIT)


def _ffn_kernel(alpha, with_ple, x_ref, w13_ref, w2_ref, g_ref, b_ref, *rest):
    if with_ple:
        p_ref, wpg_ref, wpe_ref, o_ref, ob_ref = rest
    else:
        o_ref, ob_ref = rest
    f = w2_ref.shape[0]
    x = x_ref[...]
    xb = x.astype(BF16)
    acc = None
    for s in range(f // FFN_CHUNK):
        lo = s * FFN_CHUNK
        g = _dot(xb, w13_ref[:, lo:lo + FFN_CHUNK])
        u = _dot(xb, w13_ref[:, f + lo:f + lo + FFN_CHUNK])
        h = (g * _sigmoid(g) * u).astype(BF16)
        part = _dot(h, w2_ref[lo:lo + FFN_CHUNK, :])
        acc = part if acc is None else acc + part
    y = _layer_norm(alpha * x + 0.5 * acc, g_ref[...], b_ref[...])
    if with_ple:
        gate = _sigmoid(_dot(y.astype(BF16), wpg_ref[...]))
        y = y + gate * _dot(p_ref[...].astype(BF16), wpe_ref[...])
    o_ref[...] = y
    ob_ref[...] = y.astype(BF16)


def _ffn(x, w13, w2, g, b, alpha, ple=None, tm=512):
    t, d = x.shape
    tm = min(tm, t)
    const = lambda a: pl.BlockSpec(a.shape, lambda i: (0, 0), pipeline_mode=pl.Buffered(1))
    in_specs = [pl.BlockSpec((tm, d), lambda i: (i, 0)), const(w13), const(w2), const(g), const(b)]
    args = [x, w13, w2, g, b]
    if ple is not None:
        p, wpg, wpe = ple
        in_specs += [pl.BlockSpec((tm, p.shape[1]), lambda i: (i, 0)), const(wpg), const(wpe)]
        args += [p, wpg, wpe]
    return pl.pallas_call(
        functools.partial(_ffn_kernel, alpha, ple is not None),
        grid=(t // tm,),
        in_specs=in_specs,
        out_specs=[pl.BlockSpec((tm, d), lambda i: (i, 0)),
                   pl.BlockSpec((tm, d), lambda i: (i, 0))],
        out_shape=[jax.ShapeDtypeStruct((t, d), F32), jax.ShapeDtypeStruct((t, d), BF16)],
        compiler_params=_params("parallel"),
        name="ffn_ln_ple" if ple is not None else "ffn_ln",
    )(*args)


def _proj_kernel(tiles_per_seq, hb_ref, wg_ref, ws_ref, wrqk_ref, wrv_ref, cw_ref,
                 xg_ref, sb_ref, rqk_ref, rv_ref, xpad_scr):
    tm = hb_ref.shape[0]

    @pl.when(pl.program_id(0) % tiles_per_seq == 0)
    def _():
        xpad_scr[0:CONV_HALO, :] = jnp.zeros((CONV_HALO, xpad_scr.shape[1]), F32)

    hb = hb_ref[...]
    for c in range(wg_ref.shape[1] // PROJ_CHUNK):
        cs = slice(c * PROJ_CHUNK, (c + 1) * PROJ_CHUNK)
        xpad_scr[CONV_HALO:CONV_HALO + tm, cs] = _dot(hb, wg_ref[:, cs])
        acc = cw_ref[GDN_CONV - 1:GDN_CONV, cs] * xpad_scr[CONV_HALO:CONV_HALO + tm, cs]
        for k in range(GDN_CONV - 1):
            off = CONV_HALO - (GDN_CONV - 1) + k
            acc = acc + cw_ref[k:k + 1, cs] * xpad_scr[off:off + tm, cs]
        xpad_scr[0:CONV_HALO, cs] = xpad_scr[tm:tm + CONV_HALO, cs]
        xg_ref[:, cs] = acc * _sigmoid(acc)
    sb_ref[...] = _dot(hb, ws_ref[...]).astype(BF16)
    rqk_ref[...] = _dot(hb, wrqk_ref[...])
    rv_ref[...] = _dot(hb, wrv_ref[...]).astype(BF16)


def _proj(hb, wg, ws, wrqk, wrv, conv_w, s, tm=512):
    t, d = hb.shape
    tm = min(tm, s)
    ws_ = [wg, ws, wrqk, wrv]
    dts = [F32, BF16, F32, BF16]
    return pl.pallas_call(
        functools.partial(_proj_kernel, s // tm),
        grid=(t // tm,),
        in_specs=[pl.BlockSpec((tm, d), lambda i: (i, 0))]
        + [pl.BlockSpec(w.shape, lambda i: (0, 0)) for w in ws_ + [conv_w]],
        out_specs=[pl.BlockSpec((tm, w.shape[1]), lambda i: (i, 0)) for w in ws_],
        out_shape=[jax.ShapeDtypeStruct((t, w.shape[1]), dt) for w, dt in zip(ws_, dts)],
        scratch_shapes=[pltpu.VMEM((tm + CONV_HALO, wg.shape[1]), F32)],
        compiler_params=_params("arbitrary"),
        name="mixer_in_proj",
    )(hb, *ws_, conv_w)


def _mm3(a, b):
    return _dot(a[0], b[0]) + _dot(a[0], b[1]) + _dot(a[1], b[0])


def _unit_lower_inverse(lows, ii, jj):
    eye = (ii == jj).astype(F32)
    diag8 = (ii >> 3) == (jj >> 3)
    xs = [_split(jnp.where(diag8, -low, 0.0)) for low in lows]
    x2s = [_split(_mm3(x, x)) for x in xs]
    ts = [eye + jnp.where(diag8, -low, 0.0) for low in lows]
    ts = [t + _mm3(_split(t), x2) for t, x2 in zip(ts, x2s)]
    x4s = [_split(_mm3(x2, x2)) for x2 in x2s]
    ts = [t + _mm3(_split(t), x4) for t, x4 in zip(ts, x4s)]
    shift = 3
    while (1 << shift) < CHUNK:
        pair = ((ii >> (shift + 1)) == (jj >> (shift + 1))) & ((ii >> shift) != (jj >> shift))
        tbs = [t.astype(BF16) for t in ts]
        mids = [_dot(jnp.where(pair, low, 0.0).astype(BF16), tb).astype(BF16) for low, tb in zip(lows, tbs)]
        ts = [t - _dot(tb, mid) for t, tb, mid in zip(ts, tbs, mids)]
        shift += 1
    return ts


def _gdn_kernel(hb_ref, qkv_ref, wb_ref, wa_ref, wat_ref, alog_r_ref, dtb_r_ref,
                alog_c_ref, dtb_c_ref, nw_ref, o_ref, state_scr):
    tb = qkv_ref.shape[1]
    width = HEADS * HEAD_DIM
    qkv_scr = qkv_ref.at[0]

    @pl.when(pl.program_id(1) == 0)
    def _():
        state_scr[...] = jnp.zeros_like(state_scr)

    ii = lax.broadcasted_iota(jnp.int32, (CHUNK, CHUNK), 0)
    jj = lax.broadcasted_iota(jnp.int32, (CHUNK, CHUNK), 1)
    tri_incl = (ii >= jj).astype(BF16)
    tri_incl_t = (ii <= jj).astype(BF16)
    neg_a_r = -jnp.exp(alog_r_ref[...])
    neg_a_c = -jnp.exp(alog_c_ref[...])
    nw = nw_ref[...]
    sq = (CHUNK, CHUNK)
    heads = range(HEADS)

    def l2n(t):
        return t * lax.rsqrt(jnp.sum(t * t, axis=-1, keepdims=True) + 1e-6)

    def local(rows):
        pairs = [(r, h) for r in rows for h in heads]
        cols, cum_rows, s_decs = [], [], []
        for r in rows:
            hb_c = hb_ref[0, r, :]
            beta_col = _sigmoid(_dot(hb_c, wb_ref[...]))
            g_col = neg_a_r * _softplus(_dot(hb_c, wa_ref[...]) + dtb_r_ref[...])
            g_row = neg_a_c * _softplus(_dot_nt(wat_ref[...], hb_c) + dtb_c_ref[...])
            gch, gcl = _split(g_col)
            grh, grl = _split(g_row)
            cum_col = _dot(tri_incl, gch) + _dot(tri_incl, gcl)
            cum_rows.append(_dot(grh, tri_incl_t) + _dot(grl, tri_incl_t))
            cum_last = cum_col[CHUNK - 1:CHUNK, :]
            cols.append((beta_col, cum_col, jnp.exp(cum_col), jnp.exp(cum_last - cum_col)))
            s_decs.append(jnp.exp(cum_last))

        n = len(pairs)
        col = lambda j, which, h: jnp.broadcast_to(cols[j][which][:, h:h + 1], sq)
        chunk_of = [j for j in range(len(rows)) for _ in heads]
        head_of = [h for _ in rows for h in heads]
        q = [l2n(qkv_scr[r, h * HEAD_DIM:(h + 1) * HEAD_DIM]) * (HEAD_DIM ** -0.5) for r, h in pairs]
        k = [l2n(qkv_scr[r, width + h * HEAD_DIM:width + (h + 1) * HEAD_DIM]) for r, h in pairs]
        v = [qkv_scr[r, 2 * width + h * HEAD_DIM:2 * width + (h + 1) * HEAD_DIM] for r, h in pairs]
        beta = [col(chunk_of[p], 0, head_of[p]) for p in range(n)]
        e_g = [col(chunk_of[p], 2, head_of[p]) for p in range(n)]
        dec = [jnp.exp(jnp.where(ii >= jj, col(chunk_of[p], 1, head_of[p])
                                 - jnp.broadcast_to(cum_rows[chunk_of[p]][head_of[p]:head_of[p] + 1, :], sq), -1e30))
               for p in range(n)]
        kb = [k[p] * beta[p] for p in range(n)]
        kbf = [k[p].astype(BF16) for p in range(n)]
        low = [jnp.where(ii > jj, _dot_nt(kb[p].astype(BF16), kbf[p]) * dec[p], 0.0) for p in range(n)]
        t_inv = _unit_lower_inverse(low, ii, jj)
        rhs = [jnp.concatenate([v[p] * beta[p], kb[p] * e_g[p]], axis=1).astype(BF16) for p in range(n)]
        sol = [_dot(t_inv[p].astype(BF16), rhs[p]) for p in range(n)]
        a_qk = [(_dot_nt(q[p].astype(BF16), kbf[p]) * dec[p]).astype(BF16) for p in range(n)]
        q_dec = [(q[p] * e_g[p]).astype(BF16) for p in range(n)]
        k_dec = [(k[p] * col(chunk_of[p], 3, head_of[p])).astype(BF16) for p in range(n)]
        return sol, a_qk, q_dec, k_dec, s_decs

    def recurrent(rows, j, sol, a_qk, q_dec, k_dec, s_decs):
        p0 = j * HEADS
        state = [state_scr[h] for h in heads]
        sbf = [s.astype(BF16) for s in state]
        v_new = [(sol[p0 + h][:, :HEAD_DIM] - _dot(sol[p0 + h][:, HEAD_DIM:].astype(BF16), sbf[h])).astype(BF16)
                 for h in heads]
        o = [_dot(q_dec[p0 + h], sbf[h]) + _dot(a_qk[p0 + h], v_new[h]) for h in heads]
        for h in heads:
            state_scr[h] = state[h] * s_decs[j][:, h:h + 1] + _dot_tn(k_dec[p0 + h], v_new[h])
        for h in heads:
            on = o[h] * lax.rsqrt(jnp.mean(o[h] * o[h], axis=-1, keepdims=True) + NORM_EPS) * nw
            o_ref[0, rows, h * HEAD_DIM:(h + 1) * HEAD_DIM] = on

    def chunk_group(c, carry):
        rows = [pl.ds(pl.multiple_of((c * GDN_GROUP + j) * CHUNK, CHUNK), CHUNK) for j in range(GDN_GROUP)]
        parts = local(rows)
        for j, r in enumerate(rows):
            recurrent(r, j, *parts)
        return carry

    lax.fori_loop(0, tb // (CHUNK * GDN_GROUP), chunk_group, 0)


def _gdn(hb, xg, wb, wa, a_log, dt_bias, norm_w, tb=512):
    bsz, s, d = hb.shape
    tb = min(tb, s)
    cw = xg.shape[2]
    full = lambda a: pl.BlockSpec(a.shape, lambda b, i: (0,) * a.ndim)
    wat = wa.T
    smalls = [wb, wa, wat, a_log.reshape(1, HEADS), dt_bias.reshape(1, HEADS),
              a_log.reshape(HEADS, 1), dt_bias.reshape(HEADS, 1), norm_w.reshape(1, HEAD_DIM)]
    return pl.pallas_call(
        _gdn_kernel,
        grid=(bsz, s // tb),
        in_specs=[pl.BlockSpec((1, tb, d), lambda b, i: (b, i, 0)),
                  pl.BlockSpec((1, tb, cw), lambda b, i: (b, i, 0))] + [full(a) for a in smalls],
        out_specs=pl.BlockSpec((1, tb, HEADS * HEAD_DIM), lambda b, i: (b, i, 0)),
        out_shape=jax.ShapeDtypeStruct((bsz, s, HEADS * HEAD_DIM), F32),
        scratch_shapes=[pltpu.VMEM((HEADS, HEAD_DIM, HEAD_DIM), F32)],
        compiler_params=_params("parallel", "arbitrary"),
        name="gated_deltanet",
    )(hb, xg, *smalls)


def _sb_kernel(q_ref, k_ref, v_ref, o_ref, out_scr, spent_scr):
    tq = q_ref.shape[1]
    tk = tq
    qi = pl.program_id(1)
    heads = range(HEADS)
    hs = lambda h: slice(h * HEAD_DIM, (h + 1) * HEAD_DIM)
    row = lax.broadcasted_iota(jnp.int32, (tq, tk), 0)
    col = lax.broadcasted_iota(jnp.int32, (tq, tk), 1)
    suffix = (row >= col).astype(BF16)
    out_scr[...] = jnp.zeros_like(out_scr)
    spent_scr[...] = jnp.zeros_like(spent_scr)
    q = [q_ref[0, :, hs(h)] for h in heads]

    def cond(carry):
        kj, spent_min = carry
        return jnp.logical_and(kj >= 0, spent_min < SB_STICK_EXHAUSTED)

    def body(carry):
        kj, _ = carry
        keys = pl.ds(pl.multiple_of(kj * tk, tk), tk)
        valid = (col + (kj - qi) * tk) < row
        z = [_dot_nt(q[h], k_ref[0, keys, hs(h)]) for h in heads]
        sp = [_softplus(z[h]) for h in heads]
        nlk = [jnp.where(valid, sp[h], 0.0) for h in heads]
        parts = [_split(nlk[h]) for h in heads]
        sums = [_dot(parts[h][0], suffix) + _dot(parts[h][1], suffix) for h in heads]
        spent = [spent_scr[h] for h in heads]
        wide = lambda a: jnp.concatenate([a] * (tk // HEAD_DIM), axis=1)
        log_w = [(z[h] - sp[h]) - (sums[h] - nlk[h]) - wide(spent[h]) for h in heads]
        wgt = [jnp.where(valid, jnp.exp(log_w[h]), 0.0).astype(BF16) for h in heads]
        for h in heads:
            out_scr[:, hs(h)] += _dot(wgt[h], v_ref[0, keys, hs(h)])
        spent = [spent[h] + jnp.broadcast_to(sums[h][:, 0:1], (tq, HEAD_DIM)) for h in heads]
        for h in heads:
            spent_scr[h] = spent[h]
        least = jnp.minimum(jnp.minimum(spent[0], spent[1]), jnp.minimum(spent[2], spent[3]))
        return kj - 1, jnp.min(least)

    lax.while_loop(cond, body, (qi, jnp.float32(0.0)))
    o_ref[0] = out_scr[...].astype(o_ref.dtype)


def _stick_breaking(qkv, tq=256):
    bsz, s, _ = qkv.shape
    tq = min(tq, s)
    hw = HEADS * HEAD_DIM
    resident = lambda j: pl.BlockSpec((1, s, hw), lambda b, i: (b, 0, j), pipeline_mode=pl.Buffered(1))
    return pl.pallas_call(
        _sb_kernel,
        grid=(bsz, s // tq),
        in_specs=[pl.BlockSpec((1, tq, hw), lambda b, i: (b, i, 0)), resident(1), resident(2)],
        out_specs=pl.BlockSpec((1, tq, hw), lambda b, i: (b, i, 0)),
        out_shape=jax.ShapeDtypeStruct((bsz, s, hw), BF16),
        scratch_shapes=[pltpu.VMEM((tq, hw), F32), pltpu.VMEM((HEADS, tq, HEAD_DIM), F32)],
        compiler_params=_params("parallel", "arbitrary"),
        name="stick_breaking",
    )(qkv, qkv, qkv)


def _ret_kernel(qk_ref, v_ref, cos_ref, sin_ref, nw_ref, o_ref, state_scr):
    tb = qk_ref.shape[1]
    width = HEADS * HEAD_DIM

    @pl.when(pl.program_id(1) == 0)
    def _():
        state_scr[...] = jnp.zeros_like(state_scr)

    ii = lax.broadcasted_iota(jnp.int32, (CHUNK, CHUNK), 0)
    jj = lax.broadcasted_iota(jnp.int32, (CHUNK, CHUNK), 1)
    rel = (ii - jj).astype(F32)
    pos = ii.astype(F32)
    nw = nw_ref[...]

    def rot(t, cos, sin):
        return t * cos + pltpu.roll(t, HEAD_DIM // 2, 1) * sin

    def chunk(c, carry):
        r0 = pl.multiple_of(c * CHUNK, CHUNK)
        cos = cos_ref[pl.ds(r0, CHUNK), :]
        sin = sin_ref[pl.ds(r0, CHUNK), :]
        for h in range(HEADS):
            lg = math.log(1.0 - 2.0 ** (-5.0 - h))
            lo = h * HEAD_DIM
            q = rot(qk_ref[0, pl.ds(r0, CHUNK), lo:lo + HEAD_DIM], cos, sin)
            k = rot(qk_ref[0, pl.ds(r0, CHUNK), width + lo:width + lo + HEAD_DIM], cos, sin) * (HEAD_DIM ** -0.5)
            v = v_ref[0, pl.ds(r0, CHUNK), h * RET_DV:(h + 1) * RET_DV]
            dmat = jnp.where(rel >= 0, jnp.exp(jnp.maximum(rel, 0.0) * lg), 0.0)
            a = _dot_nt(q.astype(BF16), k.astype(BF16)) * dmat
            q_dec = jnp.exp((pos + 1.0) * lg)
            k_dec = jnp.exp((CHUNK - 1.0 - pos) * lg)
            state = state_scr[h]
            o = _dot(a.astype(BF16), v) + _dot((q * q_dec).astype(BF16), state.astype(BF16))
            state_scr[h] = state * math.exp(CHUNK * lg) + _dot_tn((k * k_dec).astype(BF16), v)
            mu = jnp.mean(o, axis=-1, keepdims=True)
            dev = o - mu
            var = jnp.mean(dev * dev, axis=-1, keepdims=True)
            o_ref[0, pl.ds(r0, CHUNK), h * RET_DV:(h + 1) * RET_DV] = dev * lax.rsqrt(var + NORM_EPS) * nw
        return carry

    lax.fori_loop(0, tb // CHUNK, chunk, 0)


def _retention(rqk, rv, cos, sin, norm_w, tb=512):
    bsz, s, _ = rqk.shape
    tb = min(tb, s)
    return pl.pallas_call(
        _ret_kernel,
        grid=(bsz, s // tb),
        in_specs=[pl.BlockSpec((1, tb, 2 * HEADS * HEAD_DIM), lambda b, i: (b, i, 0)),
                  pl.BlockSpec((1, tb, HEADS * RET_DV), lambda b, i: (b, i, 0)),
                  pl.BlockSpec((tb, HEAD_DIM), lambda b, i: (i, 0)),
                  pl.BlockSpec((tb, HEAD_DIM), lambda b, i: (i, 0)),
                  pl.BlockSpec((1, RET_DV), lambda b, i: (0, 0))],
        out_specs=pl.BlockSpec((1, tb, HEADS * RET_DV), lambda b, i: (b, i, 0)),
        out_shape=jax.ShapeDtypeStruct((bsz, s, HEADS * RET_DV), F32),
        scratch_shapes=[pltpu.VMEM((HEADS, HEAD_DIM, RET_DV), F32)],
        compiler_params=_params("parallel", "arbitrary"),
        name="retention",
    )(rqk, rv, cos, sin, norm_w.reshape(1, RET_DV))


def _merge_kernel(alpha, x_ref, hb_ref, oa_ref, ob_ref, oc_ref, wgz_ref, wrg_ref, wma_ref, wmb_ref, wmc_ref,
                  wbg_ref, wbs_ref, wbr_ref, wo_ref, g_ref, b_ref, o_ref, obf_ref):
    hb = hb_ref[...]

    def silu(t):
        return t * _sigmoid(t)

    oa = (oa_ref[...] * silu(_dot(hb, wgz_ref[...]))).astype(BF16)
    oc = (oc_ref[...] * silu(_dot(hb, wrg_ref[...]))).astype(BF16)
    ob = ob_ref[...]
    mix = None
    for c in range(wo_ref.shape[0] // MERGE_CHUNK):
        cs = slice(c * MERGE_CHUNK, (c + 1) * MERGE_CHUNK)
        merged = _sigmoid(_dot(hb, wma_ref[:, cs])) * _dot(oa, wbg_ref[:, cs])
        merged = merged + _sigmoid(_dot(hb, wmb_ref[:, cs])) * _dot(ob, wbs_ref[:, cs])
        merged = merged + _sigmoid(_dot(hb, wmc_ref[:, cs])) * _dot(oc, wbr_ref[:, cs])
        part = _dot(merged.astype(BF16), wo_ref[cs, :])
        mix = part if mix is None else mix + part
    y = _layer_norm(alpha * x_ref[...] + mix, g_ref[...], b_ref[...])
    o_ref[...] = y
    obf_ref[...] = y.astype(BF16)


def _merge(x, hb, oa, ob, oc, weights, g, b, alpha, tm=512):
    t, d = x.shape
    tm = min(tm, t)
    acts = [x, hb, oa, ob, oc]
    consts = list(weights) + [g, b]
    return pl.pallas_call(
        functools.partial(_merge_kernel, alpha),
        grid=(t // tm,),
        in_specs=[pl.BlockSpec((tm, a.shape[1]), lambda i: (i, 0)) for a in acts]
        + [pl.BlockSpec(w.shape, lambda i: (0, 0), pipeline_mode=pl.Buffered(1)) for w in consts],
        out_specs=[pl.BlockSpec((tm, d), lambda i: (i, 0)), pl.BlockSpec((tm, d), lambda i: (i, 0))],
        out_shape=[jax.ShapeDtypeStruct((t, d), F32), jax.ShapeDtypeStruct((t, d), BF16)],
        compiler_params=_params("parallel"),
        name="mixer_merge_ln",
    )(*acts, *consts)


def _rotary_tables(s):
    half = HEAD_DIM // 2
    inv = ROPE_BASE ** (-jnp.arange(half, dtype=F32) / half)
    ang = jnp.arange(s).astype(F32)[:, None] * inv[None, :]
    cos, sin = jnp.cos(ang), jnp.sin(ang)
    return jnp.concatenate([cos, cos], axis=-1), jnp.concatenate([-sin, sin], axis=-1)


def kernel(x, p, ffn1_w13, ffn1_w2, w_in, gdn_conv_w, gdn_a_log, gdn_dt_bias, gdn_norm_w, ret_norm_w, w_br_gdn, w_br_sb, w_br_ret, w_out, ffn2_w13, ffn2_w2, ln_g, ln_b, w_ple_gate, w_ple_proj):
    bsz, s, d = x.shape
    depth = ffn1_w13.shape[0]
    t = bsz * s
    d_ff = ffn1_w2.shape[1]
    alpha = (2 * depth) ** 0.25
    hw = HEADS * HEAD_DIM
    rv_w = HEADS * RET_DV
    o_gq, o_gz, o_gb, o_ga = 0, 3 * hw, 4 * hw, 4 * hw + HEADS
    o_sq = 4 * hw + 2 * HEADS
    o_rq = o_sq + 3 * hw
    o_rv = o_rq + 2 * hw
    o_rg = o_rv + rv_w
    o_ma = o_rg + rv_w
    bf = lambda a: a.astype(BF16)
    cos, sin = _rotary_tables(s)

    xf = x.reshape(t, d)
    for i in range(depth):
        wi = w_in[i]
        row = lambda a: a.reshape(1, d)
        xf, xb = _ffn(xf, bf(ffn1_w13[i]), bf(ffn1_w2[i]), row(ln_g[i, 0]), row(ln_b[i, 0]), alpha)
        w_sb = jnp.concatenate([wi[:, o_sq:o_sq + hw] * (HEAD_DIM ** -0.5), wi[:, o_sq + hw:o_sq + 3 * hw]], axis=1)
        xg, sqkv, rqk, rv = _proj(xb, bf(wi[:, o_gq:o_gq + 3 * hw]), bf(w_sb),
                                  bf(wi[:, o_rq:o_rq + 2 * hw]), bf(wi[:, o_rv:o_rv + rv_w]), gdn_conv_w[i], s)
        hb3 = xb.reshape(bsz, s, d)
        oa = _gdn(hb3, xg.reshape(bsz, s, 3 * hw), bf(wi[:, o_gb:o_gb + HEADS]),
                  bf(wi[:, o_ga:o_ga + HEADS]), gdn_a_log[i], gdn_dt_bias[i], gdn_norm_w[i])
        ob = _stick_breaking(sqkv.reshape(bsz, s, 3 * hw))
        oc = _retention(rqk.reshape(bsz, s, 2 * hw), rv.reshape(bsz, s, rv_w), cos, sin, ret_norm_w[i])
        weights = [bf(wi[:, o_gz:o_gz + hw]), bf(wi[:, o_rg:o_rg + rv_w]),
                   bf(wi[:, o_ma:o_ma + d]), bf(wi[:, o_ma + d:o_ma + 2 * d]), bf(wi[:, o_ma + 2 * d:o_ma + 3 * d]),
                   bf(w_br_gdn[i]), bf(w_br_sb[i]), bf(w_br_ret[i]), bf(w_out[i])]
        xf, xb = _merge(xf, xb, oa.reshape(t, hw), ob.reshape(t, hw), oc.reshape(t, rv_w), weights,
                        row(ln_g[i, 1]), row(ln_b[i, 1]), alpha)
        xf, xb = _ffn(xf, bf(ffn2_w13[i]), bf(ffn2_w2[i]), row(ln_g[i, 2]), row(ln_b[i, 2]), alpha,
                      ple=(p[i].reshape(t, p.shape[-1]), bf(w_ple_gate[i]), bf(w_ple_proj[i])))
    return xf.reshape(bsz, s, d)
```

```python
import functools
import math

import jax
import jax.numpy as jnp
from jax import lax
from jax.experimental import pallas as pl
from jax.experimental.pallas import tpu as pltpu

F32 = jnp.float32
BF16 = jnp.bfloat16

NORM_EPS = 1e-5
HEADS = 4
HEAD_DIM = 128
RET_DV = 256
GDN_CONV = 4
ROPE_BASE = 10000.0
CHUNK = 128
CONV_HALO = 8
GDN_GROUP = 4
MERGE_CHUNK = 256
FFN_CHUNK = 256
PROJ_CHUNK = 256
SB_STICK_EXHAUSTED = 110.0
VMEM_LIMIT = 56 * 1024 * 1024

HW = HEADS * HEAD_DIM
RV_W = HEADS * RET_DV


def _dot(a, b):
    return jnp.dot(a, b, preferred_element_type=F32)


def _dot_nt(a, b):
    return lax.dot_general(a, b, (((1,), (1,)), ((), ())), preferred_element_type=F32)


def _dot_tn(a, b):
    return lax.dot_general(a, b, (((0,), (0,)), ((), ())), preferred_element_type=F32)


def _split(a):
    hi = a.astype(BF16)
    lo = (a - hi.astype(F32)).astype(BF16)
    return hi, lo


def _mm3(a, b):
    return _dot(a[0], b[0]) + _dot(a[0], b[1]) + _dot(a[1], b[0])


def _softplus(x):
    return jnp.maximum(x, 0.0) + jnp.log(1.0 + jnp.exp(-jnp.abs(x)))


def _sigmoid(x):
    return 1.0 / (1.0 + jnp.exp(-x))


def _layer_norm(y, g, b):
    mu = jnp.mean(y, axis=-1, keepdims=True)
    d = y - mu
    var = jnp.mean(d * d, axis=-1, keepdims=True)
    return d * lax.rsqrt(var + NORM_EPS) * g + b


def _params(*sem):
    return pltpu.CompilerParams(dimension_semantics=sem, vmem_limit_bytes=VMEM_LIMIT)


def _layer_spec(a, layer):
    zeros = (0,) * (a.ndim - 1)
    return pl.BlockSpec((None,) + a.shape[1:], lambda *_: (layer,) + zeros, pipeline_mode=pl.Buffered(1))


def _ffn_kernel(alpha, with_ple, x_ref, w13_ref, w2_ref, g_ref, b_ref, *rest):
    if with_ple:
        p_ref, wpg_ref, wpe_ref, o_ref, ob_ref = rest
    else:
        o_ref, ob_ref = rest
    f = w2_ref.shape[0]
    x = x_ref[...]
    xb = x.astype(BF16)
    acc = None
    for s in range(f // FFN_CHUNK):
        lo = s * FFN_CHUNK
        g = _dot(xb, w13_ref[:, lo:lo + FFN_CHUNK])
        u = _dot(xb, w13_ref[:, f + lo:f + lo + FFN_CHUNK])
        h = (g * _sigmoid(g) * u).astype(BF16)
        part = _dot(h, w2_ref[lo:lo + FFN_CHUNK, :])
        acc = part if acc is None else acc + part
    y = _layer_norm(alpha * x + 0.5 * acc, g_ref[...], b_ref[...])
    if with_ple:
        gate = _sigmoid(_dot(y.astype(BF16), wpg_ref[...]))
        y = y + gate * _dot(p_ref[...].astype(BF16), wpe_ref[...])
    o_ref[...] = y
    ob_ref[...] = y.astype(BF16)


def _ffn(x, layer, w13, w2, g, b, alpha, ple=None, tm=512):
    t, d = x.shape
    tm = min(tm, t)
    consts = [w13, w2, g, b]
    in_specs = [pl.BlockSpec((tm, d), lambda i: (i, 0))] + [_layer_spec(a, layer) for a in consts]
    args = [x] + consts
    if ple is not None:
        p, wpg, wpe = ple
        in_specs += [pl.BlockSpec((None, tm, p.shape[2]), lambda i: (layer, i, 0)),
                     _layer_spec(wpg, layer), _layer_spec(wpe, layer)]
        args += [p, wpg, wpe]
    return pl.pallas_call(
        functools.partial(_ffn_kernel, alpha, ple is not None),
        grid=(t // tm,),
        in_specs=in_specs,
        out_specs=[pl.BlockSpec((tm, d), lambda i: (i, 0)),
                   pl.BlockSpec((tm, d), lambda i: (i, 0))],
        out_shape=[jax.ShapeDtypeStruct((t, d), F32), jax.ShapeDtypeStruct((t, d), BF16)],
        compiler_params=_params("parallel"),
        name="ffn_ln_ple" if ple is not None else "ffn_ln",
    )(*args)


def _proj_kernel(tiles_per_seq, hb_ref, w_ref, cw_ref, xg_ref, sb_ref, rqk_ref, rv_ref, xpad_scr):
    tm = hb_ref.shape[0]
    gw = xg_ref.shape[1]

    @pl.when(pl.program_id(0) % tiles_per_seq == 0)
    def _():
        xpad_scr[0:CONV_HALO, :] = jnp.zeros((CONV_HALO, xpad_scr.shape[1]), F32)

    hb = hb_ref[...]
    def conv_chunk(c):
        cs = slice(c * PROJ_CHUNK, (c + 1) * PROJ_CHUNK)
        xpad_scr[CONV_HALO:CONV_HALO + tm, cs] = _dot(hb, w_ref[:, cs])
        acc = cw_ref[GDN_CONV - 1:GDN_CONV, cs] * xpad_scr[CONV_HALO:CONV_HALO + tm, cs]
        for k in range(GDN_CONV - 1):
            off = CONV_HALO - (GDN_CONV - 1) + k
            acc = acc + cw_ref[k:k + 1, cs] * xpad_scr[off:off + tm, cs]
        xpad_scr[0:CONV_HALO, cs] = xpad_scr[tm:tm + CONV_HALO, cs]
        xg_ref[:, cs] = acc * _sigmoid(acc)

    plain = []
    lo = gw
    for ref in (sb_ref, rqk_ref, rv_ref):
        for c in range(ref.shape[1] // PROJ_CHUNK):
            plain.append((ref, c * PROJ_CHUNK, lo + c * PROJ_CHUNK))
        lo += ref.shape[1]
    n_conv = gw // PROJ_CHUNK
    per = -(-len(plain) // n_conv)
    for c in range(n_conv):
        conv_chunk(c)
        for ref, dst, src in plain[c * per:(c + 1) * per]:
            ref[:, dst:dst + PROJ_CHUNK] = _dot(hb, w_ref[:, src:src + PROJ_CHUNK]).astype(ref.dtype)


def _proj(hb, layer, w, conv_w, s, tm=512):
    t, d = hb.shape
    tm = min(tm, s)
    widths = [3 * HW, 3 * HW, 2 * HW, RV_W]
    dts = [F32, BF16, F32, BF16]
    return pl.pallas_call(
        functools.partial(_proj_kernel, s // tm),
        grid=(t // tm,),
        in_specs=[pl.BlockSpec((tm, d), lambda i: (i, 0)), _layer_spec(w, layer), _layer_spec(conv_w, layer)],
        out_specs=[pl.BlockSpec((tm, n), lambda i: (i, 0)) for n in widths],
        out_shape=[jax.ShapeDtypeStruct((t, n), dt) for n, dt in zip(widths, dts)],
        scratch_shapes=[pltpu.VMEM((tm + CONV_HALO, widths[0]), F32)],
        compiler_params=_params("arbitrary"),
        name="mixer_in_proj",
    )(hb, w, conv_w)


def _unit_lower_inverse(lows, ii, jj):
    eye = (ii == jj).astype(F32)
    diag8 = (ii >> 3) == (jj >> 3)
    xs = [_split(jnp.where(diag8, -low, 0.0)) for low in lows]
    x2s = [_split(_mm3(x, x)) for x in xs]
    ts = [eye + jnp.where(diag8, -low, 0.0) for low in lows]
    ts = [t + _mm3(_split(t), x2) for t, x2 in zip(ts, x2s)]
    x4s = [_split(_mm3(x2, x2)) for x2 in x2s]
    ts = [t + _mm3(_split(t), x4) for t, x4 in zip(ts, x4s)]
    shift = 3
    while (1 << shift) < CHUNK:
        pair = ((ii >> (shift + 1)) == (jj >> (shift + 1))) & ((ii >> shift) != (jj >> shift))
        tbs = [t.astype(BF16) for t in ts]
        mids = [_dot(jnp.where(pair, low, 0.0).astype(BF16), tb).astype(BF16) for low, tb in zip(lows, tbs)]
        ts = [t - _dot(tb, mid) for t, tb, mid in zip(ts, tbs, mids)]
        shift += 1
    return ts


def _gdn_kernel(hb_ref, qkv_ref, wb_ref, wa_ref, wat_ref, alog_r_ref, dtb_r_ref,
                alog_c_ref, dtb_c_ref, nw_ref, o_ref, state_scr):
    tb = qkv_ref.shape[1]
    qkv = qkv_ref.at[0]

    @pl.when(pl.program_id(1) == 0)
    def _():
        state_scr[...] = jnp.zeros_like(state_scr)

    ii = lax.broadcasted_iota(jnp.int32, (CHUNK, CHUNK), 0)
    jj = lax.broadcasted_iota(jnp.int32, (CHUNK, CHUNK), 1)
    tri_incl = (ii >= jj).astype(BF16)
    tri_incl_t = (ii <= jj).astype(BF16)
    neg_a_r = -jnp.exp(alog_r_ref[...])
    neg_a_c = -jnp.exp(alog_c_ref[...])
    nw = nw_ref[...]
    sq = (CHUNK, CHUNK)
    heads = range(HEADS)

    def l2n(t):
        return t * lax.rsqrt(jnp.sum(t * t, axis=-1, keepdims=True) + 1e-6)

    def local(rows):
        pairs = [(r, h) for r in rows for h in heads]
        cols, cum_rows, s_decs = [], [], []
        for r in rows:
            hb_c = hb_ref[0, r, :]
            beta_col = _sigmoid(_dot(hb_c, wb_ref[...]))
            g_col = neg_a_r * _softplus(_dot(hb_c, wa_ref[...]) + dtb_r_ref[...])
            g_row = neg_a_c * _softplus(_dot_nt(wat_ref[...], hb_c) + dtb_c_ref[...])
            gch, gcl = _split(g_col)
            grh, grl = _split(g_row)
            cum_col = _dot(tri_incl, gch) + _dot(tri_incl, gcl)
            cum_rows.append(_dot(grh, tri_incl_t) + _dot(grl, tri_incl_t))
            cum_last = cum_col[CHUNK - 1:CHUNK, :]
            cols.append((beta_col, cum_col, jnp.exp(cum_col), jnp.exp(cum_last - cum_col)))
            s_decs.append(jnp.exp(cum_last))

        n = len(pairs)
        col = lambda j, which, h: jnp.broadcast_to(cols[j][which][:, h:h + 1], sq)
        chunk_of = [j for j in range(len(rows)) for _ in heads]
        head_of = [h for _ in rows for h in heads]
        q = [l2n(qkv[r, h * HEAD_DIM:(h + 1) * HEAD_DIM]) * (HEAD_DIM ** -0.5) for r, h in pairs]
        k = [l2n(qkv[r, HW + h * HEAD_DIM:HW + (h + 1) * HEAD_DIM]) for r, h in pairs]
        v = [qkv[r, 2 * HW + h * HEAD_DIM:2 * HW + (h + 1) * HEAD_DIM] for r, h in pairs]
        beta = [col(chunk_of[p], 0, head_of[p]) for p in range(n)]
        e_g = [col(chunk_of[p], 2, head_of[p]) for p in range(n)]
        dec = [jnp.exp(jnp.where(ii >= jj, col(chunk_of[p], 1, head_of[p])
                                 - jnp.broadcast_to(cum_rows[chunk_of[p]][head_of[p]:head_of[p] + 1, :], sq), -1e30))
               for p in range(n)]
        kb = [k[p] * beta[p] for p in range(n)]
        kbf = [k[p].astype(BF16) for p in range(n)]
        low = [jnp.where(ii > jj, _dot_nt(kb[p].astype(BF16), kbf[p]) * dec[p], 0.0) for p in range(n)]
        t_inv = _unit_lower_inverse(low, ii, jj)
        rhs = [jnp.concatenate([v[p] * beta[p], kb[p] * e_g[p]], axis=1).astype(BF16) for p in range(n)]
        sol = [_dot(t_inv[p].astype(BF16), rhs[p]) for p in range(n)]
        a_qk = [(_dot_nt(q[p].astype(BF16), kbf[p]) * dec[p]).astype(BF16) for p in range(n)]
        q_dec = [(q[p] * e_g[p]).astype(BF16) for p in range(n)]
        k_dec = [(k[p] * col(chunk_of[p], 3, head_of[p])).astype(BF16) for p in range(n)]
        return sol, a_qk, q_dec, k_dec, s_decs

    def recurrent(rows, j, sol, a_qk, q_dec, k_dec, s_decs):
        p0 = j * HEADS
        state = [state_scr[h] for h in heads]
        sbf = [s.astype(BF16) for s in state]
        v_new = [(sol[p0 + h][:, :HEAD_DIM] - _dot(sol[p0 + h][:, HEAD_DIM:].astype(BF16), sbf[h])).astype(BF16)
                 for h in heads]
        o = [_dot(q_dec[p0 + h], sbf[h]) + _dot(a_qk[p0 + h], v_new[h]) for h in heads]
        for h in heads:
            state_scr[h] = state[h] * s_decs[j][:, h:h + 1] + _dot_tn(k_dec[p0 + h], v_new[h])
        for h in heads:
            on = o[h] * lax.rsqrt(jnp.mean(o[h] * o[h], axis=-1, keepdims=True) + NORM_EPS) * nw
            o_ref[0, rows, h * HEAD_DIM:(h + 1) * HEAD_DIM] = on

    def chunk_group(c, carry):
        rows = [pl.ds(pl.multiple_of((c * GDN_GROUP + j) * CHUNK, CHUNK), CHUNK) for j in range(GDN_GROUP)]
        parts = local(rows)
        for j, r in enumerate(rows):
            recurrent(r, j, *parts)
        return carry

    lax.fori_loop(0, tb // (CHUNK * GDN_GROUP), chunk_group, 0)


def _gdn(hb, xg, layer, smalls, tb=512):
    bsz, s, d = hb.shape
    tb = min(tb, s)
    return pl.pallas_call(
        _gdn_kernel,
        grid=(bsz, s // tb),
        in_specs=[pl.BlockSpec((1, tb, d), lambda b, i: (b, i, 0)),
                  pl.BlockSpec((1, tb, 3 * HW), lambda b, i: (b, i, 0))] + [_layer_spec(a, layer) for a in smalls],
        out_specs=pl.BlockSpec((1, tb, HW), lambda b, i: (b, i, 0)),
        out_shape=jax.ShapeDtypeStruct((bsz, s, HW), F32),
        scratch_shapes=[pltpu.VMEM((HEADS, HEAD_DIM, HEAD_DIM), F32)],
        compiler_params=_params("parallel", "arbitrary"),
        name="gated_deltanet",
    )(hb, xg, *smalls)


def _sb_kernel(q_ref, k_ref, v_ref, o_ref, out_scr, spent_scr):
    tq = q_ref.shape[1]
    tk = tq
    qi = pl.program_id(1)
    heads = range(HEADS)
    hs = lambda h: slice(h * HEAD_DIM, (h + 1) * HEAD_DIM)
    row = lax.broadcasted_iota(jnp.int32, (tq, tk), 0)
    col = lax.broadcasted_iota(jnp.int32, (tq, tk), 1)
    suffix = (row >= col).astype(BF16)
    q = [q_ref[0, :, hs(h)] for h in heads]
    wide = lambda a: jnp.concatenate([a] * (tk // HEAD_DIM), axis=1)

    def block(kj, spent, valid):
        keys = pl.ds(pl.multiple_of(kj * tk, tk), tk)
        z = [_dot_nt(q[h], k_ref[0, keys, hs(h)]) for h in heads]
        sp = [_softplus(z[h]) for h in heads]
        nlk = sp if valid is None else [jnp.where(valid, sp[h], 0.0) for h in heads]
        parts = [_split(nlk[h]) for h in heads]
        sums = [_dot(parts[h][0], suffix) + _dot(parts[h][1], suffix) for h in heads]
        wgt = [jnp.exp((z[h] - sp[h]) - (sums[h] - nlk[h]) - wide(spent[h])) for h in heads]
        if valid is not None:
            wgt = [jnp.where(valid, wgt[h], 0.0) for h in heads]
        pv = [_dot(wgt[h].astype(BF16), v_ref[0, keys, hs(h)]) for h in heads]
        spent = [spent[h] + jnp.broadcast_to(sums[h][:, 0:1], (tq, HEAD_DIM)) for h in heads]
        least = jnp.minimum(jnp.minimum(spent[0], spent[1]), jnp.minimum(spent[2], spent[3]))
        return pv, spent, jnp.min(least)

    zero = jnp.zeros((tq, HEAD_DIM), F32)
    pv, spent, least = block(qi, [zero] * HEADS, col < row)
    for h in heads:
        out_scr[:, hs(h)] = pv[h]
        spent_scr[h] = spent[h]

    def cond(carry):
        kj, spent_min = carry
        return jnp.logical_and(kj >= 0, spent_min < SB_STICK_EXHAUSTED)

    def body(carry):
        kj, _ = carry
        pv, spent, least = block(kj, [spent_scr[h] for h in heads], None)
        for h in heads:
            out_scr[:, hs(h)] += pv[h]
            spent_scr[h] = spent[h]
        return kj - 1, least

    lax.while_loop(cond, body, (qi - 1, least))
    o_ref[0] = out_scr[...].astype(o_ref.dtype)


def _stick_breaking(qkv, tq=256):
    bsz, s, _ = qkv.shape
    tq = min(tq, s)
    resident = lambda j: pl.BlockSpec((1, s, HW), lambda b, i: (b, 0, j), pipeline_mode=pl.Buffered(1))
    return pl.pallas_call(
        _sb_kernel,
        grid=(bsz, s // tq),
        in_specs=[pl.BlockSpec((1, tq, HW), lambda b, i: (b, i, 0)), resident(1), resident(2)],
        out_specs=pl.BlockSpec((1, tq, HW), lambda b, i: (b, i, 0)),
        out_shape=jax.ShapeDtypeStruct((bsz, s, HW), BF16),
        scratch_shapes=[pltpu.VMEM((tq, HW), F32), pltpu.VMEM((HEADS, tq, HEAD_DIM), F32)],
        compiler_params=_params("parallel", "arbitrary"),
        name="stick_breaking",
    )(qkv, qkv, qkv)


def _ret_kernel(qk_ref, v_ref, cos_ref, sin_ref, nw_ref, o_ref, state_scr):
    tb = qk_ref.shape[1]
    heads = range(HEADS)

    @pl.when(pl.program_id(1) == 0)
    def _():
        state_scr[...] = jnp.zeros_like(state_scr)

    ii = lax.broadcasted_iota(jnp.int32, (CHUNK, CHUNK), 0)
    jj = lax.broadcasted_iota(jnp.int32, (CHUNK, CHUNK), 1)
    rel = (ii - jj).astype(F32)
    pos = ii.astype(F32)
    nw = nw_ref[...]
    log_gamma = [math.log(1.0 - 2.0 ** (-5.0 - h)) for h in heads]
    dmat = [jnp.where(rel >= 0, jnp.exp(jnp.maximum(rel, 0.0) * lg), 0.0) for lg in log_gamma]
    q_dec = [jnp.exp((pos + 1.0) * lg) for lg in log_gamma]
    k_dec = [jnp.exp((CHUNK - 1.0 - pos) * lg) * (HEAD_DIM ** -0.5) for lg in log_gamma]

    def rot(t, cos, sin):
        return t * cos + pltpu.roll(t, HEAD_DIM // 2, 1) * sin

    def chunk(c, carry):
        rows = pl.ds(pl.multiple_of(c * CHUNK, CHUNK), CHUNK)
        cos = cos_ref[rows, :]
        sin = sin_ref[rows, :]
        q = [rot(qk_ref[0, rows, h * HEAD_DIM:(h + 1) * HEAD_DIM], cos, sin) for h in heads]
        k = [rot(qk_ref[0, rows, HW + h * HEAD_DIM:HW + (h + 1) * HEAD_DIM], cos, sin) for h in heads]
        v = [v_ref[0, rows, h * RET_DV:(h + 1) * RET_DV] for h in heads]
        a = [(_dot_nt(q[h].astype(BF16), (k[h] * (HEAD_DIM ** -0.5)).astype(BF16)) * dmat[h]).astype(BF16)
             for h in heads]
        state = [state_scr[h] for h in heads]
        o = [_dot(a[h], v[h]) + _dot((q[h] * q_dec[h]).astype(BF16), state[h].astype(BF16)) for h in heads]
        for h in heads:
            state_scr[h] = state[h] * math.exp(CHUNK * log_gamma[h]) + _dot_tn((k[h] * k_dec[h]).astype(BF16), v[h])
        for h in heads:
            mu = jnp.mean(o[h], axis=-1, keepdims=True)
            dev = o[h] - mu
            var = jnp.mean(dev * dev, axis=-1, keepdims=True)
            o_ref[0, rows, h * RET_DV:(h + 1) * RET_DV] = dev * lax.rsqrt(var + NORM_EPS) * nw
        return carry

    lax.fori_loop(0, tb // CHUNK, chunk, 0)


def _retention(rqk, rv, cos, sin, layer, norm_w, tb=512):
    bsz, s, _ = rqk.shape
    tb = min(tb, s)
    return pl.pallas_call(
        _ret_kernel,
        grid=(bsz, s // tb),
        in_specs=[pl.BlockSpec((1, tb, 2 * HW), lambda b, i: (b, i, 0)),
                  pl.BlockSpec((1, tb, RV_W), lambda b, i: (b, i, 0)),
                  pl.BlockSpec((tb, HEAD_DIM), lambda b, i: (i, 0)),
                  pl.BlockSpec((tb, HEAD_DIM), lambda b, i: (i, 0)),
                  _layer_spec(norm_w, layer)],
        out_specs=pl.BlockSpec((1, tb, RV_W), lambda b, i: (b, i, 0)),
        out_shape=jax.ShapeDtypeStruct((bsz, s, RV_W), F32),
        scratch_shapes=[pltpu.VMEM((HEADS, HEAD_DIM, RET_DV), F32)],
        compiler_params=_params("parallel", "arbitrary"),
        name="retention",
    )(rqk, rv, cos, sin, norm_w)


def _merge_kernel(alpha, x_ref, hb_ref, oa_ref, ob_ref, oc_ref, wg_ref, wbr_ref, wo_ref, g_ref, b_ref,
                  o_ref, obf_ref):
    hb = hb_ref[...]
    d = wo_ref.shape[0]
    o_ma = HW + RV_W

    def silu(t):
        return t * _sigmoid(t)

    oa = (oa_ref[...] * silu(_dot(hb, wg_ref[:, 0:HW]))).astype(BF16)
    oc = (oc_ref[...] * silu(_dot(hb, wg_ref[:, HW:o_ma]))).astype(BF16)
    ob = ob_ref[...]
    mix = None
    for c in range(d // MERGE_CHUNK):
        lo = c * MERGE_CHUNK
        cs = slice(lo, lo + MERGE_CHUNK)
        gate = lambda j: _sigmoid(_dot(hb, wg_ref[:, o_ma + j * d + lo:o_ma + j * d + lo + MERGE_CHUNK]))
        merged = gate(0) * _dot(oa, wbr_ref[0:HW, cs])
        merged = merged + gate(1) * _dot(ob, wbr_ref[HW:2 * HW, cs])
        merged = merged + gate(2) * _dot(oc, wbr_ref[2 * HW:2 * HW + RV_W, cs])
        part = _dot(merged.astype(BF16), wo_ref[cs, :])
        mix = part if mix is None else mix + part
    y = _layer_norm(alpha * x_ref[...] + mix, g_ref[...], b_ref[...])
    o_ref[...] = y
    obf_ref[...] = y.astype(BF16)


def _merge(x, hb, oa, ob, oc, layer, consts, alpha, tm=512):
    t, d = x.shape
    tm = min(tm, t)
    acts = [x, hb, oa, ob, oc]
    return pl.pallas_call(
        functools.partial(_merge_kernel, alpha),
        grid=(t // tm,),
        in_specs=[pl.BlockSpec((tm, a.shape[1]), lambda i: (i, 0)) for a in acts]
        + [_layer_spec(w, layer) for w in consts],
        out_specs=[pl.BlockSpec((tm, d), lambda i: (i, 0)), pl.BlockSpec((tm, d), lambda i: (i, 0))],
        out_shape=[jax.ShapeDtypeStruct((t, d), F32), jax.ShapeDtypeStruct((t, d), BF16)],
        compiler_params=_params("parallel"),
        name="mixer_merge_ln",
    )(*acts, *consts)


def _rotary_tables(s):
    half = HEAD_DIM // 2
    inv = ROPE_BASE ** (-jnp.arange(half, dtype=F32) / half)
    ang = jnp.arange(s).astype(F32)[:, None] * inv[None, :]
    cos, sin = jnp.cos(ang), jnp.sin(ang)
    return jnp.concatenate([cos, cos], axis=-1), jnp.concatenate([-sin, sin], axis=-1)


def kernel(x, p, ffn1_w13, ffn1_w2, w_in, gdn_conv_w, gdn_a_log, gdn_dt_bias, gdn_norm_w, ret_norm_w, w_br_gdn, w_br_sb, w_br_ret, w_out, ffn2_w13, ffn2_w2, ln_g, ln_b, w_ple_gate, w_ple_proj):
    bsz, s, d = x.shape
    depth = ffn1_w13.shape[0]
    t = bsz * s
    alpha = (2 * depth) ** 0.25
    bf = lambda a: a.astype(BF16)
    o_gz, o_gb, o_ga = 3 * HW, 4 * HW, 4 * HW + HEADS
    o_sq = 4 * HW + 2 * HEADS
    o_rq = o_sq + 3 * HW
    o_rv = o_rq + 2 * HW
    o_rg = o_rv + RV_W
    o_ma = o_rg + RV_W
    cols = lambda lo, n: w_in[:, :, lo:lo + n]
    w_proj = bf(jnp.concatenate([cols(0, 3 * HW), cols(o_sq, HW) * (HEAD_DIM ** -0.5), cols(o_sq + HW, 2 * HW),
                                 cols(o_rq, 2 * HW), cols(o_rv, RV_W)], axis=-1))
    w_gate = bf(jnp.concatenate([cols(o_gz, HW), cols(o_rg, RV_W), cols(o_ma, 3 * d)], axis=-1))
    w_branch = bf(jnp.concatenate([w_br_gdn, w_br_sb, w_br_ret], axis=1))
    wa = bf(cols(o_ga, HEADS))
    gdn_smalls = [bf(cols(o_gb, HEADS)), wa, jnp.swapaxes(wa, 1, 2),
                  gdn_a_log.reshape(depth, 1, HEADS), gdn_dt_bias.reshape(depth, 1, HEADS),
                  gdn_a_log.reshape(depth, HEADS, 1), gdn_dt_bias.reshape(depth, HEADS, 1),
                  gdn_norm_w.reshape(depth, 1, HEAD_DIM)]
    ret_nw = ret_norm_w.reshape(depth, 1, RET_DV)
    ffn1 = (bf(ffn1_w13), bf(ffn1_w2))
    ffn2 = (bf(ffn2_w13), bf(ffn2_w2))
    w_o, w_pg, w_pe = bf(w_out), bf(w_ple_gate), bf(w_ple_proj)
    ln = lambda a, j: a[:, j].reshape(depth, 1, d)
    p2 = p.reshape(depth, t, p.shape[-1])
    cos, sin = _rotary_tables(s)

    xf = x.reshape(t, d)
    for i in range(depth):
        xf, xb = _ffn(xf, i, *ffn1, ln(ln_g, 0), ln(ln_b, 0), alpha)
        xg, sqkv, rqk, rv = _proj(xb, i, w_proj, gdn_conv_w, s)
        oa = _gdn(xb.reshape(bsz, s, d), xg.reshape(bsz, s, 3 * HW), i, gdn_smalls)
        ob = _stick_breaking(sqkv.reshape(bsz, s, 3 * HW))
        oc = _retention(rqk.reshape(bsz, s, 2 * HW), rv.reshape(bsz, s, RV_W), cos, sin, i, ret_nw)
        xf, xb = _merge(xf, xb, oa.reshape(t, HW), ob.reshape(t, HW), oc.reshape(t, RV_W), i,
                        [w_gate, w_branch, w_o, ln(ln_g, 1), ln(ln_b, 1)], alpha)
        xf, xb = _ffn(xf, i, *ffn2, ln(ln_g, 2), ln(ln_b, 2), alpha, ple=(p2, w_pg, w_pe))
    return xf.reshape(bsz, s, d)
```

```python
import functools
import math

import jax
import jax.numpy as jnp
from jax import lax
from jax.experimental import pallas as pl
from jax.experimental.pallas import tpu as pltpu

F32 = jnp.float32
BF16 = jnp.bfloat16

NORM_EPS = 1e-5
HEADS = 4
HEAD_DIM = 128
RET_DV = 256
GDN_CONV = 4
ROPE_BASE = 10000.0
CHUNK = 128
CONV_HALO = 8
GDN_GROUP = 4
MERGE_CHUNK = 256
FFN_CHUNK = 256
PROJ_CHUNK = 256
SB_STICK_EXHAUSTED = 110.0
VMEM_LIMIT = 56 * 1024 * 1024

HW = HEADS * HEAD_DIM
RV_W = HEADS * RET_DV


def _dot(a, b):
    return jnp.dot(a, b, preferred_element_type=F32)


def _dot_nt(a, b):
    return lax.dot_general(a, b, (((1,), (1,)), ((), ())), preferred_element_type=F32)


def _dot_tn(a, b):
    return lax.dot_general(a, b, (((0,), (0,)), ((), ())), preferred_element_type=F32)


def _split(a):
    hi = a.astype(BF16)
    lo = (a - hi.astype(F32)).astype(BF16)
    return hi, lo


def _mm3(a, b):
    return _dot(a[0], b[0]) + _dot(a[0], b[1]) + _dot(a[1], b[0])


def _softplus(x):
    return jnp.maximum(x, 0.0) + jnp.log(1.0 + jnp.exp(-jnp.abs(x)))


def _sigmoid(x):
    return 1.0 / (1.0 + jnp.exp(-x))


def _layer_norm(y, g, b):
    mu = jnp.mean(y, axis=-1, keepdims=True)
    d = y - mu
    var = jnp.mean(d * d, axis=-1, keepdims=True)
    return d * lax.rsqrt(var + NORM_EPS) * g + b


def _params(*sem):
    return pltpu.CompilerParams(dimension_semantics=sem, vmem_limit_bytes=VMEM_LIMIT)


def _layer_spec(a, layer):
    zeros = (0,) * (a.ndim - 1)
    return pl.BlockSpec((None,) + a.shape[1:], lambda *_: (layer,) + zeros, pipeline_mode=pl.Buffered(1))


def _ffn_kernel(alpha, with_ple, x_ref, w13_ref, w2_ref, g_ref, b_ref, *rest):
    if with_ple:
        p_ref, wpg_ref, wpe_ref, o_ref, ob_ref = rest
    else:
        o_ref, ob_ref = rest
    f = w2_ref.shape[0]
    x = x_ref[...]
    xb = x.astype(BF16)
    acc = None
    for s in range(f // FFN_CHUNK):
        lo = s * FFN_CHUNK
        g = _dot(xb, w13_ref[:, lo:lo + FFN_CHUNK])
        u = _dot(xb, w13_ref[:, f + lo:f + lo + FFN_CHUNK])
        h = (g * _sigmoid(g) * u).astype(BF16)
        part = _dot(h, w2_ref[lo:lo + FFN_CHUNK, :])
        acc = part if acc is None else acc + part
    y = _layer_norm(alpha * x + 0.5 * acc, g_ref[...], b_ref[...])
    if with_ple:
        gate = _sigmoid(_dot(y.astype(BF16), wpg_ref[...]))
        y = y + gate * _dot(p_ref[...].astype(BF16), wpe_ref[...])
    o_ref[...] = y
    ob_ref[...] = y.astype(BF16)


def _ffn(x, layer, w13, w2, g, b, alpha, ple=None, tm=512):
    t, d = x.shape
    tm = min(tm, t)
    consts = [w13, w2, g, b]
    in_specs = [pl.BlockSpec((tm, d), lambda i: (i, 0))] + [_layer_spec(a, layer) for a in consts]
    args = [x] + consts
    if ple is not None:
        p, wpg, wpe = ple
        in_specs += [pl.BlockSpec((None, tm, p.shape[2]), lambda i: (layer, i, 0)),
                     _layer_spec(wpg, layer), _layer_spec(wpe, layer)]
        args += [p, wpg, wpe]
    return pl.pallas_call(
        functools.partial(_ffn_kernel, alpha, ple is not None),
        grid=(t // tm,),
        in_specs=in_specs,
        out_specs=[pl.BlockSpec((tm, d), lambda i: (i, 0)),
                   pl.BlockSpec((tm, d), lambda i: (i, 0))],
        out_shape=[jax.ShapeDtypeStruct((t, d), F32), jax.ShapeDtypeStruct((t, d), BF16)],
        compiler_params=_params("parallel"),
        name="ffn_ln_ple" if ple is not None else "ffn_ln",
    )(*args)


def _proj_kernel(tiles_per_seq, hb_ref, w_ref, cw_ref, xg_ref, sb_ref, rqk_ref, rv_ref, xpad_scr):
    tm = hb_ref.shape[0]
    gw = xg_ref.shape[1]

    @pl.when(pl.program_id(0) % tiles_per_seq == 0)
    def _():
        xpad_scr[0:CONV_HALO, :] = jnp.zeros((CONV_HALO, xpad_scr.shape[1]), F32)

    hb = hb_ref[...]
    def conv_chunk(c):
        cs = slice(c * PROJ_CHUNK, (c + 1) * PROJ_CHUNK)
        xpad_scr[CONV_HALO:CONV_HALO + tm, cs] = _dot(hb, w_ref[:, cs])
        acc = cw_ref[GDN_CONV - 1:GDN_CONV, cs] * xpad_scr[CONV_HALO:CONV_HALO + tm, cs]
        for k in range(GDN_CONV - 1):
            off = CONV_HALO - (GDN_CONV - 1) + k
            acc = acc + cw_ref[k:k + 1, cs] * xpad_scr[off:off + tm, cs]
        xpad_scr[0:CONV_HALO, cs] = xpad_scr[tm:tm + CONV_HALO, cs]
        xg_ref[:, cs] = acc * _sigmoid(acc)

    plain = []
    lo = gw
    for ref in (sb_ref, rqk_ref, rv_ref):
        for c in range(ref.shape[1] // PROJ_CHUNK):
            plain.append((ref, c * PROJ_CHUNK, lo + c * PROJ_CHUNK))
        lo += ref.shape[1]
    n_conv = gw // PROJ_CHUNK
    per = -(-len(plain) // n_conv)
    for c in range(n_conv):
        conv_chunk(c)
        for ref, dst, src in plain[c * per:(c + 1) * per]:
            ref[:, dst:dst + PROJ_CHUNK] = _dot(hb, w_ref[:, src:src + PROJ_CHUNK]).astype(ref.dtype)


def _proj(hb, layer, w, conv_w, s, tm=512):
    t, d = hb.shape
    tm = min(tm, s)
    widths = [3 * HW, 3 * HW, 2 * HW, RV_W]
    dts = [F32, BF16, F32, BF16]
    return pl.pallas_call(
        functools.partial(_proj_kernel, s // tm),
        grid=(t // tm,),
        in_specs=[pl.BlockSpec((tm, d), lambda i: (i, 0)), _layer_spec(w, layer), _layer_spec(conv_w, layer)],
        out_specs=[pl.BlockSpec((tm, n), lambda i: (i, 0)) for n in widths],
        out_shape=[jax.ShapeDtypeStruct((t, n), dt) for n, dt in zip(widths, dts)],
        scratch_shapes=[pltpu.VMEM((tm + CONV_HALO, widths[0]), F32)],
        compiler_params=_params("arbitrary"),
        name="mixer_in_proj",
    )(hb, w, conv_w)


def _unit_lower_inverse(lows, ii, jj):
    eye = (ii == jj).astype(F32)
    diag8 = (ii >> 3) == (jj >> 3)
    xs = [jnp.where(diag8, -low, 0.0) for low in lows]
    xbs = [x.astype(BF16) for x in xs]
    x2s = [_dot(xb, xb).astype(BF16) for xb in xbs]
    ts = [eye + x for x in xs]
    ts = [t + _dot(t.astype(BF16), x2) for t, x2 in zip(ts, x2s)]
    x4s = [_dot(x2, x2).astype(BF16) for x2 in x2s]
    ts = [t + _dot(t.astype(BF16), x4) for t, x4 in zip(ts, x4s)]
    shift = 3
    while (1 << shift) < CHUNK:
        pair = ((ii >> (shift + 1)) == (jj >> (shift + 1))) & ((ii >> shift) != (jj >> shift))
        tbs = [t.astype(BF16) for t in ts]
        mids = [_dot(jnp.where(pair, low, 0.0).astype(BF16), tb).astype(BF16) for low, tb in zip(lows, tbs)]
        ts = [t - _dot(tb, mid) for t, tb, mid in zip(ts, tbs, mids)]
        shift += 1
    return ts


def _gdn_kernel(hb_ref, qkv_ref, wb_ref, wa_ref, wat_ref, alog_r_ref, dtb_r_ref,
                alog_c_ref, dtb_c_ref, nw_ref, o_ref, state_scr):
    tb = qkv_ref.shape[1]
    qkv = qkv_ref.at[0]

    @pl.when(pl.program_id(1) == 0)
    def _():
        state_scr[...] = jnp.zeros_like(state_scr)

    ii = lax.broadcasted_iota(jnp.int32, (CHUNK, CHUNK), 0)
    jj = lax.broadcasted_iota(jnp.int32, (CHUNK, CHUNK), 1)
    tri_incl = (ii >= jj).astype(BF16)
    tri_incl_t = (ii <= jj).astype(BF16)
    neg_a_r = -jnp.exp(alog_r_ref[...])
    neg_a_c = -jnp.exp(alog_c_ref[...])
    nw = nw_ref[...]
    sq = (CHUNK, CHUNK)
    heads = range(HEADS)

    def l2n(t):
        return t * lax.rsqrt(jnp.sum(t * t, axis=-1, keepdims=True) + 1e-6)

    def local(rows):
        pairs = [(r, h) for r in rows for h in heads]
        cols, cum_rows, s_decs = [], [], []
        for r in rows:
            hb_c = hb_ref[0, r, :]
            beta_col = _sigmoid(_dot(hb_c, wb_ref[...]))
            g_col = neg_a_r * _softplus(_dot(hb_c, wa_ref[...]) + dtb_r_ref[...])
            g_row = neg_a_c * _softplus(_dot_nt(wat_ref[...], hb_c) + dtb_c_ref[...])
            gch, gcl = _split(g_col)
            grh, grl = _split(g_row)
            cum_col = _dot(tri_incl, gch) + _dot(tri_incl, gcl)
            cum_rows.append(_dot(grh, tri_incl_t) + _dot(grl, tri_incl_t))
            cum_last = cum_col[CHUNK - 1:CHUNK, :]
            cols.append((beta_col, cum_col, jnp.exp(cum_col), jnp.exp(cum_last - cum_col)))
            s_decs.append(jnp.exp(cum_last))

        n = len(pairs)
        col = lambda j, which, h: jnp.broadcast_to(cols[j][which][:, h:h + 1], sq)
        chunk_of = [j for j in range(len(rows)) for _ in heads]
        head_of = [h for _ in rows for h in heads]
        q = [l2n(qkv[r, h * HEAD_DIM:(h + 1) * HEAD_DIM]) * (HEAD_DIM ** -0.5) for r, h in pairs]
        k = [l2n(qkv[r, HW + h * HEAD_DIM:HW + (h + 1) * HEAD_DIM]) for r, h in pairs]
        v = [qkv[r, 2 * HW + h * HEAD_DIM:2 * HW + (h + 1) * HEAD_DIM] for r, h in pairs]
        beta = [col(chunk_of[p], 0, head_of[p]) for p in range(n)]
        e_g = [col(chunk_of[p], 2, head_of[p]) for p in range(n)]
        dec = [jnp.exp(jnp.where(ii >= jj, col(chunk_of[p], 1, head_of[p])
                                 - jnp.broadcast_to(cum_rows[chunk_of[p]][head_of[p]:head_of[p] + 1, :], sq), -1e30))
               for p in range(n)]
        kb = [k[p] * beta[p] for p in range(n)]
        kbf = [k[p].astype(BF16) for p in range(n)]
        kq = [_dot_nt(jnp.concatenate([kb[p], q[p]], axis=0).astype(BF16), kbf[p]) for p in range(n)]
        low = [jnp.where(ii > jj, kq[p][:CHUNK] * dec[p], 0.0) for p in range(n)]
        t_inv = _unit_lower_inverse(low, ii, jj)
        rhs = [jnp.concatenate([v[p] * beta[p], kb[p] * e_g[p]], axis=1).astype(BF16) for p in range(n)]
        sol = [_dot(t_inv[p].astype(BF16), rhs[p]) for p in range(n)]
        a_qk = [(kq[p][CHUNK:] * dec[p]).astype(BF16) for p in range(n)]
        wq = [jnp.concatenate([sol[p][:, HEAD_DIM:], q[p] * e_g[p]], axis=0).astype(BF16) for p in range(n)]
        u = [sol[p][:, :HEAD_DIM] for p in range(n)]
        k_dec = [(k[p] * col(chunk_of[p], 3, head_of[p])).astype(BF16) for p in range(n)]
        return u, wq, a_qk, k_dec, s_decs

    def recurrent(rows, j, u, wq, a_qk, k_dec, s_decs):
        p0 = j * HEADS
        state = [state_scr[h] for h in heads]
        from_state = [_dot(wq[p0 + h], state[h].astype(BF16)) for h in heads]
        v_new = [(u[p0 + h] - from_state[h][:CHUNK]).astype(BF16) for h in heads]
        o = [from_state[h][CHUNK:] + _dot(a_qk[p0 + h], v_new[h]) for h in heads]
        for h in heads:
            state_scr[h] = state[h] * s_decs[j][:, h:h + 1] + _dot_tn(k_dec[p0 + h], v_new[h])
        for h in heads:
            on = o[h] * lax.rsqrt(jnp.mean(o[h] * o[h], axis=-1, keepdims=True) + NORM_EPS) * nw
            o_ref[0, rows, h * HEAD_DIM:(h + 1) * HEAD_DIM] = on

    def chunk_group(c, carry):
        rows = [pl.ds(pl.multiple_of((c * GDN_GROUP + j) * CHUNK, CHUNK), CHUNK) for j in range(GDN_GROUP)]
        parts = local(rows)
        for j, r in enumerate(rows):
            recurrent(r, j, *parts)
        return carry

    lax.fori_loop(0, tb // (CHUNK * GDN_GROUP), chunk_group, 0)


def _gdn(hb, xg, layer, smalls, tb=1024):
    bsz, s, d = hb.shape
    tb = min(tb, s)
    return pl.pallas_call(
        _gdn_kernel,
        grid=(bsz, s // tb),
        in_specs=[pl.BlockSpec((1, tb, d), lambda b, i: (b, i, 0)),
                  pl.BlockSpec((1, tb, 3 * HW), lambda b, i: (b, i, 0))] + [_layer_spec(a, layer) for a in smalls],
        out_specs=pl.BlockSpec((1, tb, HW), lambda b, i: (b, i, 0)),
        out_shape=jax.ShapeDtypeStruct((bsz, s, HW), F32),
        scratch_shapes=[pltpu.VMEM((HEADS, HEAD_DIM, HEAD_DIM), F32)],
        compiler_params=_params("parallel", "arbitrary"),
        name="gated_deltanet",
    )(hb, xg, *smalls)


def _sb_kernel(q_ref, k_ref, v_ref, o_ref, out_scr, spent_scr):
    tq = q_ref.shape[1]
    tk = tq
    qi = pl.program_id(1)
    heads = range(HEADS)
    hs = lambda h: slice(h * HEAD_DIM, (h + 1) * HEAD_DIM)
    row = lax.broadcasted_iota(jnp.int32, (tq, tk), 0)
    col = lax.broadcasted_iota(jnp.int32, (tq, tk), 1)
    suffix = (row >= col).astype(BF16)
    q = [q_ref[0, :, hs(h)] for h in heads]
    wide = lambda a: jnp.concatenate([a] * (tk // HEAD_DIM), axis=1)

    def block(kj, spent, valid):
        keys = pl.ds(pl.multiple_of(kj * tk, tk), tk)
        z = [_dot_nt(q[h], k_ref[0, keys, hs(h)]) for h in heads]
        sp = [_softplus(z[h]) for h in heads]
        nlk = sp if valid is None else [jnp.where(valid, sp[h], 0.0) for h in heads]
        parts = [_split(nlk[h]) for h in heads]
        sums = [_dot(parts[h][0], suffix) + _dot(parts[h][1], suffix) for h in heads]
        wgt = [jnp.exp((z[h] - sp[h]) - (sums[h] - nlk[h]) - wide(spent[h])) for h in heads]
        if valid is not None:
            wgt = [jnp.where(valid, wgt[h], 0.0) for h in heads]
        pv = [_dot(wgt[h].astype(BF16), v_ref[0, keys, hs(h)]) for h in heads]
        spent = [spent[h] + jnp.broadcast_to(sums[h][:, 0:1], (tq, HEAD_DIM)) for h in heads]
        least = jnp.minimum(jnp.minimum(spent[0], spent[1]), jnp.minimum(spent[2], spent[3]))
        return pv, spent, jnp.min(least)

    zero = jnp.zeros((tq, HEAD_DIM), F32)
    pv, spent, least = block(qi, [zero] * HEADS, col < row)
    for h in heads:
        out_scr[:, hs(h)] = pv[h]
        spent_scr[h] = spent[h]

    def cond(carry):
        kj, spent_min = carry
        return jnp.logical_and(kj >= 0, spent_min < SB_STICK_EXHAUSTED)

    def body(carry):
        kj, _ = carry
        pv, spent, least = block(kj, [spent_scr[h] for h in heads], None)
        for h in heads:
            out_scr[:, hs(h)] += pv[h]
            spent_scr[h] = spent[h]
        return kj - 1, least

    lax.while_loop(cond, body, (qi - 1, least))
    o_ref[0] = out_scr[...].astype(o_ref.dtype)


def _stick_breaking(qkv, tq=256):
    bsz, s, _ = qkv.shape
    tq = min(tq, s)
    resident = lambda j: pl.BlockSpec((1, s, HW), lambda b, i: (b, 0, j), pipeline_mode=pl.Buffered(1))
    return pl.pallas_call(
        _sb_kernel,
        grid=(bsz, s // tq),
        in_specs=[pl.BlockSpec((1, tq, HW), lambda b, i: (b, i, 0)), resident(1), resident(2)],
        out_specs=pl.BlockSpec((1, tq, HW), lambda b, i: (b, i, 0)),
        out_shape=jax.ShapeDtypeStruct((bsz, s, HW), BF16),
        scratch_shapes=[pltpu.VMEM((tq, HW), F32), pltpu.VMEM((HEADS, tq, HEAD_DIM), F32)],
        compiler_params=_params("parallel", "arbitrary"),
        name="stick_breaking",
    )(qkv, qkv, qkv)


def _ret_kernel(qk_ref, v_ref, cos_ref, sin_ref, nw_ref, o_ref, state_scr):
    tb = qk_ref.shape[1]
    heads = range(HEADS)

    @pl.when(pl.program_id(1) == 0)
    def _():
        state_scr[...] = jnp.zeros_like(state_scr)

    ii = lax.broadcasted_iota(jnp.int32, (CHUNK, CHUNK), 0)
    jj = lax.broadcasted_iota(jnp.int32, (CHUNK, CHUNK), 1)
    rel = (ii - jj).astype(F32)
    pos = ii.astype(F32)
    nw = nw_ref[...]
    log_gamma = [math.log(1.0 - 2.0 ** (-5.0 - h)) for h in heads]
    dmat = [jnp.where(rel >= 0, jnp.exp(jnp.maximum(rel, 0.0) * lg), 0.0) for lg in log_gamma]
    q_dec = [jnp.exp((pos + 1.0) * lg) for lg in log_gamma]
    k_dec = [jnp.exp((CHUNK - 1.0 - pos) * lg) * (HEAD_DIM ** -0.5) for lg in log_gamma]

    def rot(t, cos, sin):
        return t * cos + pltpu.roll(t, HEAD_DIM // 2, 1) * sin

    def chunk(c, carry):
        rows = pl.ds(pl.multiple_of(c * CHUNK, CHUNK), CHUNK)
        cos = cos_ref[rows, :]
        sin = sin_ref[rows, :]
        q = [rot(qk_ref[0, rows, h * HEAD_DIM:(h + 1) * HEAD_DIM], cos, sin) for h in heads]
        k = [rot(qk_ref[0, rows, HW + h * HEAD_DIM:HW + (h + 1) * HEAD_DIM], cos, sin) for h in heads]
        v = [v_ref[0, rows, h * RET_DV:(h + 1) * RET_DV] for h in heads]
        a = [(_dot_nt(q[h].astype(BF16), (k[h] * (HEAD_DIM ** -0.5)).astype(BF16)) * dmat[h]).astype(BF16)
             for h in heads]
        state = [state_scr[h] for h in heads]
        o = [_dot(a[h], v[h]) + _dot((q[h] * q_dec[h]).astype(BF16), state[h].astype(BF16)) for h in heads]
        for h in heads:
            state_scr[h] = state[h] * math.exp(CHUNK * log_gamma[h]) + _dot_tn((k[h] * k_dec[h]).astype(BF16), v[h])
        for h in heads:
            mu = jnp.mean(o[h], axis=-1, keepdims=True)
            dev = o[h] - mu
            var = jnp.mean(dev * dev, axis=-1, keepdims=True)
            o_ref[0, rows, h * RET_DV:(h + 1) * RET_DV] = dev * lax.rsqrt(var + NORM_EPS) * nw
        return carry

    lax.fori_loop(0, tb // CHUNK, chunk, 0)


def _retention(rqk, rv, cos, sin, layer, norm_w, tb=1024):
    bsz, s, _ = rqk.shape
    tb = min(tb, s)
    return pl.pallas_call(
        _ret_kernel,
        grid=(bsz, s // tb),
        in_specs=[pl.BlockSpec((1, tb, 2 * HW), lambda b, i: (b, i, 0)),
                  pl.BlockSpec((1, tb, RV_W), lambda b, i: (b, i, 0)),
                  pl.BlockSpec((tb, HEAD_DIM), lambda b, i: (i, 0)),
                  pl.BlockSpec((tb, HEAD_DIM), lambda b, i: (i, 0)),
                  _layer_spec(norm_w, layer)],
        out_specs=pl.BlockSpec((1, tb, RV_W), lambda b, i: (b, i, 0)),
        out_shape=jax.ShapeDtypeStruct((bsz, s, RV_W), F32),
        scratch_shapes=[pltpu.VMEM((HEADS, HEAD_DIM, RET_DV), F32)],
        compiler_params=_params("parallel", "arbitrary"),
        name="retention",
    )(rqk, rv, cos, sin, norm_w)


def _merge_kernel(alpha, x_ref, hb_ref, oa_ref, ob_ref, oc_ref, wg_ref, wbr_ref, wo_ref, g_ref, b_ref,
                  o_ref, obf_ref):
    hb = hb_ref[...]
    d = wo_ref.shape[0]
    o_ma = HW + RV_W

    def silu(t):
        return t * _sigmoid(t)

    oa = (oa_ref[...] * silu(_dot(hb, wg_ref[:, 0:HW]))).astype(BF16)
    oc = (oc_ref[...] * silu(_dot(hb, wg_ref[:, HW:o_ma]))).astype(BF16)
    ob = ob_ref[...]
    mix = None
    for c in range(d // MERGE_CHUNK):
        lo = c * MERGE_CHUNK
        cs = slice(lo, lo + MERGE_CHUNK)
        gate = lambda j: _sigmoid(_dot(hb, wg_ref[:, o_ma + j * d + lo:o_ma + j * d + lo + MERGE_CHUNK]))
        merged = gate(0) * _dot(oa, wbr_ref[0:HW, cs])
        merged = merged + gate(1) * _dot(ob, wbr_ref[HW:2 * HW, cs])
        merged = merged + gate(2) * _dot(oc, wbr_ref[2 * HW:2 * HW + RV_W, cs])
        part = _dot(merged.astype(BF16), wo_ref[cs, :])
        mix = part if mix is None else mix + part
    y = _layer_norm(alpha * x_ref[...] + mix, g_ref[...], b_ref[...])
    o_ref[...] = y
    obf_ref[...] = y.astype(BF16)


def _merge(x, hb, oa, ob, oc, layer, consts, alpha, tm=512):
    t, d = x.shape
    tm = min(tm, t)
    acts = [x, hb, oa, ob, oc]
    return pl.pallas_call(
        functools.partial(_merge_kernel, alpha),
        grid=(t // tm,),
        in_specs=[pl.BlockSpec((tm, a.shape[1]), lambda i: (i, 0)) for a in acts]
        + [_layer_spec(w, layer) for w in consts],
        out_specs=[pl.BlockSpec((tm, d), lambda i: (i, 0)), pl.BlockSpec((tm, d), lambda i: (i, 0))],
        out_shape=[jax.ShapeDtypeStruct((t, d), F32), jax.ShapeDtypeStruct((t, d), BF16)],
        compiler_params=_params("parallel"),
        name="mixer_merge_ln",
    )(*acts, *consts)


def _regroup_kernel(w_ref, proj_ref, gate_ref):
    o_sq = 4 * HW + 2 * HEADS
    o_rg = o_sq + 5 * HW + RV_W
    step = 4 * HEAD_DIM

    def copy(dst, d0, s0, n, scale=None):
        for c in range(0, n, step):
            m = min(step, n - c)
            v = w_ref[:, s0 + c:s0 + c + m]
            if scale is not None:
                v = v * scale
            dst[:, d0 + c:d0 + c + m] = v.astype(BF16)

    copy(proj_ref, 0, 0, 3 * HW)
    copy(proj_ref, 3 * HW, o_sq, HW, HEAD_DIM ** -0.5)
    copy(proj_ref, 4 * HW, o_sq + HW, 4 * HW + RV_W)
    copy(gate_ref, 0, 3 * HW, HW)
    copy(gate_ref, HW, o_rg, gate_ref.shape[1] - HW)


def _regroup_w_in(w_in, rows=128):
    depth, d, d_in = w_in.shape
    n_proj = 8 * HW + RV_W
    n_gate = d_in - n_proj - 2 * HEADS
    return pl.pallas_call(
        _regroup_kernel,
        grid=(depth, d // rows),
        in_specs=[pl.BlockSpec((None, rows, d_in), lambda l, i: (l, i, 0))],
        out_specs=[pl.BlockSpec((None, rows, n_proj), lambda l, i: (l, i, 0)),
                   pl.BlockSpec((None, rows, n_gate), lambda l, i: (l, i, 0))],
        out_shape=[jax.ShapeDtypeStruct((depth, d, n_proj), BF16), jax.ShapeDtypeStruct((depth, d, n_gate), BF16)],
        compiler_params=_params("parallel", "parallel"),
        name="regroup_w_in",
    )(w_in)


def _rotary_tables(s):
    half = HEAD_DIM // 2
    inv = ROPE_BASE ** (-jnp.arange(half, dtype=F32) / half)
    ang = jnp.arange(s).astype(F32)[:, None] * inv[None, :]
    cos, sin = jnp.cos(ang), jnp.sin(ang)
    return jnp.concatenate([cos, cos], axis=-1), jnp.concatenate([-sin, sin], axis=-1)


def kernel(x, p, ffn1_w13, ffn1_w2, w_in, gdn_conv_w, gdn_a_log, gdn_dt_bias, gdn_norm_w, ret_norm_w, w_br_gdn, w_br_sb, w_br_ret, w_out, ffn2_w13, ffn2_w2, ln_g, ln_b, w_ple_gate, w_ple_proj):
    bsz, s, d = x.shape
    depth = ffn1_w13.shape[0]
    t = bsz * s
    alpha = (2 * depth) ** 0.25
    bf = lambda a: a.astype(BF16)
    w_proj, w_gate = _regroup_w_in(w_in)
    w_branch = bf(jnp.concatenate([w_br_gdn, w_br_sb, w_br_ret], axis=1))
    o_gb = 4 * HW
    wb = bf(w_in[:, :, o_gb:o_gb + HEADS])
    wa = bf(w_in[:, :, o_gb + HEADS:o_gb + 2 * HEADS])
    gdn_smalls = [wb, wa, jnp.swapaxes(wa, 1, 2),
                  gdn_a_log.reshape(depth, 1, HEADS), gdn_dt_bias.reshape(depth, 1, HEADS),
                  gdn_a_log.reshape(depth, HEADS, 1), gdn_dt_bias.reshape(depth, HEADS, 1),
                  gdn_norm_w.reshape(depth, 1, HEAD_DIM)]
    ret_nw = ret_norm_w.reshape(depth, 1, RET_DV)
    ffn1 = (bf(ffn1_w13), bf(ffn1_w2))
    ffn2 = (bf(ffn2_w13), bf(ffn2_w2))
    w_o, w_pg, w_pe = bf(w_out), bf(w_ple_gate), bf(w_ple_proj)
    ln = lambda a, j: a[:, j].reshape(depth, 1, d)
    p2 = p.reshape(depth, t, p.shape[-1])
    cos, sin = _rotary_tables(s)

    xf = x.reshape(t, d)
    for i in range(depth):
        xf, xb = _ffn(xf, i, *ffn1, ln(ln_g, 0), ln(ln_b, 0), alpha)
        xg, sqkv, rqk, rv = _proj(xb, i, w_proj, gdn_conv_w, s)
        oa = _gdn(xb.reshape(bsz, s, d), xg.reshape(bsz, s, 3 * HW), i, gdn_smalls)
        ob = _stick_breaking(sqkv.reshape(bsz, s, 3 * HW))
        oc = _retention(rqk.reshape(bsz, s, 2 * HW), rv.reshape(bsz, s, RV_W), cos, sin, i, ret_nw)
        xf, xb = _merge(xf, xb, oa.reshape(t, HW), ob.reshape(t, HW), oc.reshape(t, RV_W), i,
                        [w_gate, w_branch, w_o, ln(ln_g, 1), ln(ln_b, 1)], alpha)
        xf, xb = _ffn(xf, i, *ffn2, ln(ln_g, 2), ln(ln_b, 2), alpha, ple=(p2, w_pg, w_pe))
    return xf.reshape(bsz, s, d)
```

```python
import functools
import math

import jax
import jax.numpy as jnp
from jax import lax
from jax.experimental import pallas as pl
from jax.experimental.pallas import tpu as pltpu

F32 = jnp.float32
BF16 = jnp.bfloat16

NORM_EPS = 1e-5
HEADS = 4
HEAD_DIM = 128
RET_DV = 256
GDN_CONV = 4
ROPE_BASE = 10000.0
CHUNK = 128
CONV_HALO = 8
GDN_GROUP = 2
MERGE_CHUNK = 256
FFN_CHUNK = 256
PROJ_CHUNK = 256
SB_STICK_EXHAUSTED = 110.0
VMEM_LIMIT = 56 * 1024 * 1024

HW = HEADS * HEAD_DIM
RV_W = HEADS * RET_DV


def _dot(a, b):
    return jnp.dot(a, b, preferred_element_type=F32)


def _dot_nt(a, b):
    return lax.dot_general(a, b, (((1,), (1,)), ((), ())), preferred_element_type=F32)


def _dot_tn(a, b):
    return lax.dot_general(a, b, (((0,), (0,)), ((), ())), preferred_element_type=F32)


def _split(a):
    hi = a.astype(BF16)
    lo = (a - hi.astype(F32)).astype(BF16)
    return hi, lo


def _softplus(x):
    return jnp.maximum(x, 0.0) + jnp.log(1.0 + jnp.exp(-jnp.abs(x)))


def _sigmoid(x):
    return 1.0 / (1.0 + jnp.exp(-x))


def _layer_norm(y, g, b):
    mu = jnp.mean(y, axis=-1, keepdims=True)
    d = y - mu
    var = jnp.mean(d * d, axis=-1, keepdims=True)
    return d * lax.rsqrt(var + NORM_EPS) * g + b


def _params(*sem):
    return pltpu.CompilerParams(dimension_semantics=sem, vmem_limit_bytes=VMEM_LIMIT)


def _layer_spec(a, layer):
    zeros = (0,) * (a.ndim - 1)
    return pl.BlockSpec((None,) + a.shape[1:], lambda *_: (layer,) + zeros, pipeline_mode=pl.Buffered(1))


def _ffn_kernel(alpha, with_ple, x_ref, w13_ref, w2_ref, g_ref, b_ref, *rest):
    if with_ple:
        p_ref, wpg_ref, wpe_ref, o_ref, ob_ref = rest
    else:
        o_ref, ob_ref = rest
    f = w2_ref.shape[0]
    x = x_ref[...]
    xb = x.astype(BF16)
    acc = None
    for s in range(f // FFN_CHUNK):
        lo = s * FFN_CHUNK
        g = _dot(xb, w13_ref[:, lo:lo + FFN_CHUNK])
        u = _dot(xb, w13_ref[:, f + lo:f + lo + FFN_CHUNK])
        h = (g * _sigmoid(g) * u).astype(BF16)
        part = _dot(h, w2_ref[lo:lo + FFN_CHUNK, :])
        acc = part if acc is None else acc + part
    y = _layer_norm(alpha * x + 0.5 * acc, g_ref[...], b_ref[...])
    if with_ple:
        gate = _sigmoid(_dot(y.astype(BF16), wpg_ref[...]))
        y = y + gate * _dot(p_ref[...].astype(BF16), wpe_ref[...])
    o_ref[...] = y
    ob_ref[...] = y.astype(BF16)


def _ffn(x, layer, w13, w2, g, b, alpha, ple=None, tm=512):
    t, d = x.shape
    tm = min(tm, t)
    consts = [w13, w2, g, b]
    in_specs = [pl.BlockSpec((tm, d), lambda i: (i, 0))] + [_layer_spec(a, layer) for a in consts]
    args = [x] + consts
    if ple is not None:
        p, wpg, wpe = ple
        in_specs += [pl.BlockSpec((None, tm, p.shape[2]), lambda i: (layer, i, 0)),
                     _layer_spec(wpg, layer), _layer_spec(wpe, layer)]
        args += [p, wpg, wpe]
    return pl.pallas_call(
        functools.partial(_ffn_kernel, alpha, ple is not None),
        grid=(t // tm,),
        in_specs=in_specs,
        out_specs=[pl.BlockSpec((tm, d), lambda i: (i, 0)),
                   pl.BlockSpec((tm, d), lambda i: (i, 0))],
        out_shape=[jax.ShapeDtypeStruct((t, d), F32), jax.ShapeDtypeStruct((t, d), BF16)],
        compiler_params=_params("parallel"),
        name="ffn_ln_ple" if ple is not None else "ffn_ln",
    )(*args)


def _proj_kernel(tiles_per_seq, hb_ref, w_ref, cw_ref, xg_ref, sb_ref, rqk_ref, rv_ref, xpad_scr):
    tm = hb_ref.shape[0]
    gw = xg_ref.shape[1]

    @pl.when(pl.program_id(0) % tiles_per_seq == 0)
    def _():
        xpad_scr[0:CONV_HALO, :] = jnp.zeros((CONV_HALO, xpad_scr.shape[1]), F32)

    hb = hb_ref[...]
    def conv_chunk(c):
        cs = slice(c * PROJ_CHUNK, (c + 1) * PROJ_CHUNK)
        xpad_scr[CONV_HALO:CONV_HALO + tm, cs] = _dot(hb, w_ref[:, cs])
        acc = cw_ref[GDN_CONV - 1:GDN_CONV, cs] * xpad_scr[CONV_HALO:CONV_HALO + tm, cs]
        for k in range(GDN_CONV - 1):
            off = CONV_HALO - (GDN_CONV - 1) + k
            acc = acc + cw_ref[k:k + 1, cs] * xpad_scr[off:off + tm, cs]
        xpad_scr[0:CONV_HALO, cs] = xpad_scr[tm:tm + CONV_HALO, cs]
        xg_ref[:, cs] = acc * _sigmoid(acc)

    plain = []
    lo = gw
    for ref in (sb_ref, rqk_ref, rv_ref):
        for c in range(ref.shape[1] // PROJ_CHUNK):
            plain.append((ref, c * PROJ_CHUNK, lo + c * PROJ_CHUNK))
        lo += ref.shape[1]
    n_conv = gw // PROJ_CHUNK
    per = -(-len(plain) // n_conv)
    for c in range(n_conv):
        conv_chunk(c)
        for ref, dst, src in plain[c * per:(c + 1) * per]:
            ref[:, dst:dst + PROJ_CHUNK] = _dot(hb, w_ref[:, src:src + PROJ_CHUNK]).astype(ref.dtype)


def _proj(hb, layer, w, conv_w, s, tm=512):
    t, d = hb.shape
    tm = min(tm, s)
    widths = [3 * HW, 3 * HW, 2 * HW, RV_W]
    dts = [F32, BF16, F32, BF16]
    return pl.pallas_call(
        functools.partial(_proj_kernel, s // tm),
        grid=(t // tm,),
        in_specs=[pl.BlockSpec((tm, d), lambda i: (i, 0)), _layer_spec(w, layer), _layer_spec(conv_w, layer)],
        out_specs=[pl.BlockSpec((tm, n), lambda i: (i, 0)) for n in widths],
        out_shape=[jax.ShapeDtypeStruct((t, n), dt) for n, dt in zip(widths, dts)],
        scratch_shapes=[pltpu.VMEM((tm + CONV_HALO, widths[0]), F32)],
        compiler_params=_params("arbitrary"),
        name="mixer_in_proj",
    )(hb, w, conv_w)


def _unit_lower_inverse(lows, ii, jj):
    eye = (ii == jj).astype(F32)
    diag8 = (ii >> 3) == (jj >> 3)
    xs = [jnp.where(diag8, -low, 0.0) for low in lows]
    xbs = [x.astype(BF16) for x in xs]
    x2s = [_dot(xb, xb).astype(BF16) for xb in xbs]
    ts = [eye + x for x in xs]
    ts = [t + _dot(t.astype(BF16), x2) for t, x2 in zip(ts, x2s)]
    x4s = [_dot(x2, x2).astype(BF16) for x2 in x2s]
    ts = [t + _dot(t.astype(BF16), x4) for t, x4 in zip(ts, x4s)]
    shift = 3
    while (1 << shift) < CHUNK:
        pair = ((ii >> (shift + 1)) == (jj >> (shift + 1))) & ((ii >> shift) != (jj >> shift))
        tbs = [t.astype(BF16) for t in ts]
        mids = [_dot(jnp.where(pair, low, 0.0).astype(BF16), tb).astype(BF16) for low, tb in zip(lows, tbs)]
        ts = [t - _dot(tb, mid) for t, tb, mid in zip(ts, tbs, mids)]
        shift += 1
    return ts


def _gdn_kernel(hb_ref, qkv_ref, wb_ref, wa_ref, wat_ref, alog_r_ref, dtb_r_ref,
                alog_c_ref, dtb_c_ref, nw_ref, o_ref, state_scr):
    bsz, tb, _ = qkv_ref.shape

    @pl.when(pl.program_id(0) == 0)
    def _():
        state_scr[...] = jnp.zeros_like(state_scr)

    ii = lax.broadcasted_iota(jnp.int32, (CHUNK, CHUNK), 0)
    jj = lax.broadcasted_iota(jnp.int32, (CHUNK, CHUNK), 1)
    tri_incl = (ii >= jj).astype(BF16)
    tri_incl_t = (ii <= jj).astype(BF16)
    neg_a_r = -jnp.exp(alog_r_ref[...])
    neg_a_c = -jnp.exp(alog_c_ref[...])
    nw = nw_ref[...]
    sq = (CHUNK, CHUNK)
    heads = range(HEADS)

    def l2n(t):
        return t * lax.rsqrt(jnp.sum(t * t, axis=-1, keepdims=True) + 1e-6)

    def local(rows):
        pairs = [(r, h) for r in rows for h in heads]
        cols, cum_rows, s_decs = [], [], []
        for b, r in rows:
            hb_c = hb_ref[b, r, :]
            beta_col = _sigmoid(_dot(hb_c, wb_ref[...]))
            g_col = neg_a_r * _softplus(_dot(hb_c, wa_ref[...]) + dtb_r_ref[...])
            g_row = neg_a_c * _softplus(_dot_nt(wat_ref[...], hb_c) + dtb_c_ref[...])
            gch, gcl = _split(g_col)
            grh, grl = _split(g_row)
            cum_col = _dot(tri_incl, gch) + _dot(tri_incl, gcl)
            cum_rows.append(_dot(grh, tri_incl_t) + _dot(grl, tri_incl_t))
            cum_last = cum_col[CHUNK - 1:CHUNK, :]
            cols.append((beta_col, cum_col, jnp.exp(cum_col), jnp.exp(cum_last - cum_col)))
            s_decs.append(jnp.exp(cum_last))

        n = len(pairs)
        col = lambda j, which, h: jnp.broadcast_to(cols[j][which][:, h:h + 1], sq)
        chunk_of = [j for j in range(len(rows)) for _ in heads]
        head_of = [h for _ in rows for h in heads]
        q = [l2n(qkv_ref[b, r, h * HEAD_DIM:(h + 1) * HEAD_DIM]) * (HEAD_DIM ** -0.5) for (b, r), h in pairs]
        k = [l2n(qkv_ref[b, r, HW + h * HEAD_DIM:HW + (h + 1) * HEAD_DIM]) for (b, r), h in pairs]
        v = [qkv_ref[b, r, 2 * HW + h * HEAD_DIM:2 * HW + (h + 1) * HEAD_DIM] for (b, r), h in pairs]
        beta = [col(chunk_of[p], 0, head_of[p]) for p in range(n)]
        e_g = [col(chunk_of[p], 2, head_of[p]) for p in range(n)]
        dec = [jnp.exp(jnp.where(ii >= jj, col(chunk_of[p], 1, head_of[p])
                                 - jnp.broadcast_to(cum_rows[chunk_of[p]][head_of[p]:head_of[p] + 1, :], sq), -1e30))
               for p in range(n)]
        kb = [k[p] * beta[p] for p in range(n)]
        kbf = [k[p].astype(BF16) for p in range(n)]
        kq = [_dot_nt(jnp.concatenate([kb[p], q[p]], axis=0).astype(BF16), kbf[p]) for p in range(n)]
        low = [jnp.where(ii > jj, kq[p][:CHUNK] * dec[p], 0.0) for p in range(n)]
        t_inv = _unit_lower_inverse(low, ii, jj)
        rhs = [jnp.concatenate([v[p] * beta[p], kb[p] * e_g[p]], axis=1).astype(BF16) for p in range(n)]
        sol = [_dot(t_inv[p].astype(BF16), rhs[p]) for p in range(n)]
        a_qk = [(kq[p][CHUNK:] * dec[p]).astype(BF16) for p in range(n)]
        wq = [jnp.concatenate([sol[p][:, HEAD_DIM:], q[p] * e_g[p]], axis=0).astype(BF16) for p in range(n)]
        u = [sol[p][:, :HEAD_DIM] for p in range(n)]
        k_dec = [(k[p] * col(chunk_of[p], 3, head_of[p])).astype(BF16) for p in range(n)]
        return u, wq, a_qk, k_dec, s_decs

    def recurrent(rows, c0, u, wq, a_qk, k_dec, s_decs):
        chains = [(i, h) for i in range(len(rows)) for h in heads]
        state = [state_scr[rows[i][0] * HEADS + h] for i, h in chains]
        pair = [(c0 + i) * HEADS + h for i, h in chains]
        from_state = [_dot(wq[p], s.astype(BF16)) for p, s in zip(pair, state)]
        v_new = [(u[p] - fs[:CHUNK]).astype(BF16) for p, fs in zip(pair, from_state)]
        o = [fs[CHUNK:] + _dot(a_qk[p], vn) for p, fs, vn in zip(pair, from_state, v_new)]
        for (i, h), p, s, vn in zip(chains, pair, state, v_new):
            state_scr[rows[i][0] * HEADS + h] = s * s_decs[c0 + i][:, h:h + 1] + _dot_tn(k_dec[p], vn)
        for (i, h), on in zip(chains, o):
            on = on * lax.rsqrt(jnp.mean(on * on, axis=-1, keepdims=True) + NORM_EPS) * nw
            o_ref[rows[i][0], rows[i][1], h * HEAD_DIM:(h + 1) * HEAD_DIM] = on

    def chunk_group(c, carry):
        slices = [pl.ds(pl.multiple_of((c * GDN_GROUP + j) * CHUNK, CHUNK), CHUNK) for j in range(GDN_GROUP)]
        rows = [(b, r) for r in slices for b in range(bsz)]
        parts = local(rows)
        for j in range(GDN_GROUP):
            recurrent(rows[j * bsz:(j + 1) * bsz], j * bsz, *parts)
        return carry

    lax.fori_loop(0, tb // (CHUNK * GDN_GROUP), chunk_group, 0)


def _gdn(hb, xg, layer, smalls, tb=512):
    bsz, s, d = hb.shape
    tb = min(tb, s)
    return pl.pallas_call(
        _gdn_kernel,
        grid=(s // tb,),
        in_specs=[pl.BlockSpec((bsz, tb, d), lambda i: (0, i, 0)),
                  pl.BlockSpec((bsz, tb, 3 * HW), lambda i: (0, i, 0))] + [_layer_spec(a, layer) for a in smalls],
        out_specs=pl.BlockSpec((bsz, tb, HW), lambda i: (0, i, 0)),
        out_shape=jax.ShapeDtypeStruct((bsz, s, HW), F32),
        scratch_shapes=[pltpu.VMEM((bsz * HEADS, HEAD_DIM, HEAD_DIM), F32)],
        compiler_params=_params("arbitrary"),
        name="gated_deltanet",
    )(hb, xg, *smalls)


def _sb_kernel(q_ref, k_ref, v_ref, o_ref, out_scr, spent_scr):
    tq = q_ref.shape[1]
    tk = tq
    qi = pl.program_id(1)
    heads = range(HEADS)
    hs = lambda h: slice(h * HEAD_DIM, (h + 1) * HEAD_DIM)
    row = lax.broadcasted_iota(jnp.int32, (tq, tk), 0)
    col = lax.broadcasted_iota(jnp.int32, (tq, tk), 1)
    suffix = (row >= col).astype(BF16)
    q = [q_ref[0, :, hs(h)] for h in heads]
    wide = lambda a: jnp.concatenate([a] * (tk // HEAD_DIM), axis=1)

    def block(kj, spent, valid):
        keys = pl.ds(pl.multiple_of(kj * tk, tk), tk)
        z = [_dot_nt(q[h], k_ref[0, keys, hs(h)]) for h in heads]
        sp = [_softplus(z[h]) for h in heads]
        nlk = sp if valid is None else [jnp.where(valid, sp[h], 0.0) for h in heads]
        parts = [_split(nlk[h]) for h in heads]
        sums = [_dot(parts[h][0], suffix) + _dot(parts[h][1], suffix) for h in heads]
        wgt = [jnp.exp((z[h] - sp[h]) - (sums[h] - nlk[h]) - wide(spent[h])) for h in heads]
        if valid is not None:
            wgt = [jnp.where(valid, wgt[h], 0.0) for h in heads]
        pv = [_dot(wgt[h].astype(BF16), v_ref[0, keys, hs(h)]) for h in heads]
        spent = [spent[h] + jnp.broadcast_to(sums[h][:, 0:1], (tq, HEAD_DIM)) for h in heads]
        least = jnp.minimum(jnp.minimum(spent[0], spent[1]), jnp.minimum(spent[2], spent[3]))
        return pv, spent, jnp.min(least)

    zero = jnp.zeros((tq, HEAD_DIM), F32)
    pv, spent, least = block(qi, [zero] * HEADS, col < row)
    for h in heads:
        out_scr[:, hs(h)] = pv[h]
        spent_scr[h] = spent[h]

    def cond(carry):
        kj, spent_min = carry
        return jnp.logical_and(kj >= 0, spent_min < SB_STICK_EXHAUSTED)

    def body(carry):
        kj, _ = carry
        pv, spent, least = block(kj, [spent_scr[h] for h in heads], None)
        for h in heads:
            out_scr[:, hs(h)] += pv[h]
            spent_scr[h] = spent[h]
        return kj - 1, least

    lax.while_loop(cond, body, (qi - 1, least))
    o_ref[0] = out_scr[...].astype(o_ref.dtype)


def _stick_breaking(qkv, tq=256):
    bsz, s, _ = qkv.shape
    tq = min(tq, s)
    resident = lambda j: pl.BlockSpec((1, s, HW), lambda b, i: (b, 0, j), pipeline_mode=pl.Buffered(1))
    return pl.pallas_call(
        _sb_kernel,
        grid=(bsz, s // tq),
        in_specs=[pl.BlockSpec((1, tq, HW), lambda b, i: (b, i, 0)), resident(1), resident(2)],
        out_specs=pl.BlockSpec((1, tq, HW), lambda b, i: (b, i, 0)),
        out_shape=jax.ShapeDtypeStruct((bsz, s, HW), BF16),
        scratch_shapes=[pltpu.VMEM((tq, HW), F32), pltpu.VMEM((HEADS, tq, HEAD_DIM), F32)],
        compiler_params=_params("parallel", "arbitrary"),
        name="stick_breaking",
    )(qkv, qkv, qkv)


def _merge_ret_kernel(alpha, tiles_per_seq, x_ref, hb_ref, oa_ref, ob_ref, qk_ref, v_ref, cos_ref, sin_ref,
                      rnw_ref, wg_ref, wbr_ref, wo_ref, g_ref, b_ref, o_ref, obf_ref,
                      state_scr, oc_scr, part_scr, gate_scr):
    tm = x_ref.shape[0]
    d = wo_ref.shape[0]
    heads = range(HEADS)
    o_ma = HW + RV_W

    @pl.when(pl.program_id(0) % tiles_per_seq == 0)
    def _():
        state_scr[...] = jnp.zeros_like(state_scr)

    ii = lax.broadcasted_iota(jnp.int32, (CHUNK, CHUNK), 0)
    jj = lax.broadcasted_iota(jnp.int32, (CHUNK, CHUNK), 1)
    rel = (ii - jj).astype(F32)
    pos = ii.astype(F32)
    rnw = rnw_ref[...]
    log_gamma = [math.log(1.0 - 2.0 ** (-5.0 - h)) for h in heads]
    dmat = [jnp.where(rel >= 0, jnp.exp(jnp.maximum(rel, 0.0) * lg), 0.0) for lg in log_gamma]
    q_dec = [jnp.exp((pos + 1.0) * lg) for lg in log_gamma]
    k_dec = [jnp.exp((CHUNK - 1.0 - pos) * lg) * (HEAD_DIM ** -0.5) for lg in log_gamma]

    def rot(t, cos, sin):
        return t * cos + pltpu.roll(t, HEAD_DIM // 2, 1) * sin

    def silu(t):
        return t * _sigmoid(t)

    def ret_chunk(c):
        rows = slice(c * CHUNK, (c + 1) * CHUNK)
        cos = cos_ref[rows, :]
        sin = sin_ref[rows, :]
        q = [rot(qk_ref[rows, h * HEAD_DIM:(h + 1) * HEAD_DIM], cos, sin) for h in heads]
        k = [rot(qk_ref[rows, HW + h * HEAD_DIM:HW + (h + 1) * HEAD_DIM], cos, sin) for h in heads]
        v = [v_ref[rows, h * RET_DV:(h + 1) * RET_DV] for h in heads]
        a = [(_dot_nt(q[h].astype(BF16), (k[h] * (HEAD_DIM ** -0.5)).astype(BF16)) * dmat[h]).astype(BF16)
             for h in heads]
        state = [state_scr[h] for h in heads]
        o = [_dot(a[h], v[h]) + _dot((q[h] * q_dec[h]).astype(BF16), state[h].astype(BF16)) for h in heads]
        for h in heads:
            state_scr[h] = state[h] * math.exp(CHUNK * log_gamma[h]) + _dot_tn((k[h] * k_dec[h]).astype(BF16), v[h])
        for h in heads:
            mu = jnp.mean(o[h], axis=-1, keepdims=True)
            dev = o[h] - mu
            var = jnp.mean(dev * dev, axis=-1, keepdims=True)
            oc_scr[rows, h * RET_DV:(h + 1) * RET_DV] = dev * lax.rsqrt(var + NORM_EPS) * rnw

    hb = hb_ref[...]
    oa = (oa_ref[...] * silu(_dot(hb, wg_ref[:, 0:HW]))).astype(BF16)
    ob = ob_ref[...]
    gate = lambda j, lo: _sigmoid(_dot(hb, wg_ref[:, o_ma + j * d + lo:o_ma + j * d + lo + MERGE_CHUNK]))
    n_cols = d // MERGE_CHUNK
    n_ret = tm // CHUNK
    for c in range(max(n_cols, n_ret)):
        if c < n_cols:
            lo = c * MERGE_CHUNK
            cs = slice(lo, lo + MERGE_CHUNK)
            part_scr[:, cs] = (gate(0, lo) * _dot(oa, wbr_ref[0:HW, cs])
                               + gate(1, lo) * _dot(ob, wbr_ref[HW:2 * HW, cs]))
            gate_scr[:, cs] = gate(2, lo)
        if c < n_ret:
            ret_chunk(c)
    oc = (oc_scr[...] * silu(_dot(hb, wg_ref[:, HW:o_ma]))).astype(BF16)
    mix = None
    for c in range(n_cols):
        lo = c * MERGE_CHUNK
        cs = slice(lo, lo + MERGE_CHUNK)
        merged = part_scr[:, cs] + gate_scr[:, cs] * _dot(oc, wbr_ref[2 * HW:2 * HW + RV_W, cs])
        part = _dot(merged.astype(BF16), wo_ref[cs, :])
        mix = part if mix is None else mix + part
    y = _layer_norm(alpha * x_ref[...] + mix, g_ref[...], b_ref[...])
    o_ref[...] = y
    obf_ref[...] = y.astype(BF16)


def _merge_ret(x, hb, oa, ob, rqk, rv, cos, sin, layer, consts, alpha, s, tm=512):
    t, d = x.shape
    tm = min(tm, s)
    tiles_per_seq = s // tm
    acts = [x, hb, oa, ob, rqk, rv]
    table = pl.BlockSpec((tm, HEAD_DIM), lambda i: (i % tiles_per_seq, 0))
    return pl.pallas_call(
        functools.partial(_merge_ret_kernel, alpha, tiles_per_seq),
        grid=(t // tm,),
        in_specs=[pl.BlockSpec((tm, a.shape[1]), lambda i: (i, 0)) for a in acts] + [table, table]
        + [_layer_spec(w, layer) for w in consts],
        out_specs=[pl.BlockSpec((tm, d), lambda i: (i, 0)), pl.BlockSpec((tm, d), lambda i: (i, 0))],
        out_shape=[jax.ShapeDtypeStruct((t, d), F32), jax.ShapeDtypeStruct((t, d), BF16)],
        scratch_shapes=[pltpu.VMEM((HEADS, HEAD_DIM, RET_DV), F32), pltpu.VMEM((tm, RV_W), F32),
                        pltpu.VMEM((tm, d), F32), pltpu.VMEM((tm, d), F32)],
        compiler_params=_params("arbitrary"),
        name="retention_merge_ln",
    )(*acts, cos, sin, *consts)


def _regroup_kernel(w_ref, proj_ref, gate_ref, ba_ref):
    ba_ref[...] = w_ref[:, 4 * HW:4 * HW + 2 * HEADS].astype(BF16)
    o_sq = 4 * HW + 2 * HEADS
    o_rg = o_sq + 5 * HW + RV_W
    step = 4 * HEAD_DIM

    def copy(dst, d0, s0, n, scale=None):
        for c in range(0, n, step):
            m = min(step, n - c)
            v = w_ref[:, s0 + c:s0 + c + m]
            if scale is not None:
                v = v * scale
            dst[:, d0 + c:d0 + c + m] = v.astype(BF16)

    copy(proj_ref, 0, 0, 3 * HW)
    copy(proj_ref, 3 * HW, o_sq, HW, HEAD_DIM ** -0.5)
    copy(proj_ref, 4 * HW, o_sq + HW, 4 * HW + RV_W)
    copy(gate_ref, 0, 3 * HW, HW)
    copy(gate_ref, HW, o_rg, gate_ref.shape[1] - HW)


def _regroup_w_in(w_in, rows=128):
    depth, d, d_in = w_in.shape
    n_proj = 8 * HW + RV_W
    n_gate = d_in - n_proj - 2 * HEADS
    return pl.pallas_call(
        _regroup_kernel,
        grid=(depth, d // rows),
        in_specs=[pl.BlockSpec((None, rows, d_in), lambda l, i: (l, i, 0))],
        out_specs=[pl.BlockSpec((None, rows, n), lambda l, i: (l, i, 0)) for n in (n_proj, n_gate, 2 * HEADS)],
        out_shape=[jax.ShapeDtypeStruct((depth, d, n), BF16) for n in (n_proj, n_gate, 2 * HEADS)],
        compiler_params=_params("parallel", "parallel"),
        name="regroup_w_in",
    )(w_in)


def _rotary_tables(s):
    half = HEAD_DIM // 2
    inv = ROPE_BASE ** (-jnp.arange(half, dtype=F32) / half)
    ang = jnp.arange(s).astype(F32)[:, None] * inv[None, :]
    cos, sin = jnp.cos(ang), jnp.sin(ang)
    return jnp.concatenate([cos, cos], axis=-1), jnp.concatenate([-sin, sin], axis=-1)


def kernel(x, p, ffn1_w13, ffn1_w2, w_in, gdn_conv_w, gdn_a_log, gdn_dt_bias, gdn_norm_w, ret_norm_w, w_br_gdn, w_br_sb, w_br_ret, w_out, ffn2_w13, ffn2_w2, ln_g, ln_b, w_ple_gate, w_ple_proj):
    bsz, s, d = x.shape
    depth = ffn1_w13.shape[0]
    t = bsz * s
    alpha = (2 * depth) ** 0.25
    bf = lambda a: a.astype(BF16)
    w_proj, w_gate, w_ba = _regroup_w_in(w_in)
    w_branch = bf(jnp.concatenate([w_br_gdn, w_br_sb, w_br_ret], axis=1))
    wa = w_ba[:, :, HEADS:]
    gdn_smalls = [w_ba[:, :, :HEADS], wa, jnp.swapaxes(wa, 1, 2),
                  gdn_a_log.reshape(depth, 1, HEADS), gdn_dt_bias.reshape(depth, 1, HEADS),
                  gdn_a_log.reshape(depth, HEADS, 1), gdn_dt_bias.reshape(depth, HEADS, 1),
                  gdn_norm_w.reshape(depth, 1, HEAD_DIM)]
    ret_nw = ret_norm_w.reshape(depth, 1, RET_DV)
    ffn1 = (bf(ffn1_w13), bf(ffn1_w2))
    ffn2 = (bf(ffn2_w13), bf(ffn2_w2))
    w_o, w_pg, w_pe = bf(w_out), bf(w_ple_gate), bf(w_ple_proj)
    ln = lambda a, j: a[:, j].reshape(depth, 1, d)
    p2 = p.reshape(depth, t, p.shape[-1])
    cos, sin = _rotary_tables(s)

    xf = x.reshape(t, d)
    for i in range(depth):
        xf, xb = _ffn(xf, i, *ffn1, ln(ln_g, 0), ln(ln_b, 0), alpha)
        xg, sqkv, rqk, rv = _proj(xb, i, w_proj, gdn_conv_w, s)
        oa = _gdn(xb.reshape(bsz, s, d), xg.reshape(bsz, s, 3 * HW), i, gdn_smalls)
        ob = _stick_breaking(sqkv.reshape(bsz, s, 3 * HW))
        xf, xb = _merge_ret(xf, xb, oa.reshape(t, HW), ob.reshape(t, HW), rqk, rv, cos, sin, i,
                            [ret_nw, w_gate, w_branch, w_o, ln(ln_g, 1), ln(ln_b, 1)], alpha, s)
        xf, xb = _ffn(xf, i, *ffn2, ln(ln_g, 2), ln(ln_b, 2), alpha, ple=(p2, w_pg, w_pe))
    return xf.reshape(bsz, s, d)
```

```python
import functools
import math

import jax
import jax.numpy as jnp
from jax import lax
from jax.experimental import pallas as pl
from jax.experimental.pallas import tpu as pltpu

F32 = jnp.float32
BF16 = jnp.bfloat16

NORM_EPS = 1e-5
HEADS = 4
HEAD_DIM = 128
RET_DV = 256
GDN_CONV = 4
ROPE_BASE = 10000.0
CHUNK = 128
CONV_HALO = 8
GDN_GROUP = 2
MERGE_CHUNK = 256
FFN_CHUNK = 256
PROJ_CHUNK = 256
SB_STICK_EXHAUSTED = 110.0
VMEM_LIMIT = 56 * 1024 * 1024

HW = HEADS * HEAD_DIM
RV_W = HEADS * RET_DV


def _dot(a, b):
    return jnp.dot(a, b, preferred_element_type=F32)


def _dot_nt(a, b):
    return lax.dot_general(a, b, (((1,), (1,)), ((), ())), preferred_element_type=F32)


def _dot_tn(a, b):
    return lax.dot_general(a, b, (((0,), (0,)), ((), ())), preferred_element_type=F32)


def _split(a):
    hi = a.astype(BF16)
    lo = (a - hi.astype(F32)).astype(BF16)
    return hi, lo


def _softplus(x):
    return jnp.maximum(x, 0.0) + jnp.log(1.0 + jnp.exp(-jnp.abs(x)))


def _sigmoid(x):
    return 1.0 / (1.0 + jnp.exp(-x))


def _layer_norm(y, g, b):
    mu = jnp.mean(y, axis=-1, keepdims=True)
    d = y - mu
    var = jnp.mean(d * d, axis=-1, keepdims=True)
    return d * lax.rsqrt(var + NORM_EPS) * g + b


def _params(*sem):
    return pltpu.CompilerParams(dimension_semantics=sem, vmem_limit_bytes=VMEM_LIMIT)


def _layer_spec(a, layer):
    zeros = (0,) * (a.ndim - 1)
    return pl.BlockSpec((None,) + a.shape[1:], lambda *_: (layer,) + zeros, pipeline_mode=pl.Buffered(1))


def _ffn_kernel(alpha, with_ple, x_ref, w13_ref, w2_ref, g_ref, b_ref, *rest):
    if with_ple:
        p_ref, wpg_ref, wpe_ref, o_ref, ob_ref = rest
    else:
        o_ref, ob_ref = rest
    f = w2_ref.shape[0]
    x = x_ref[...]
    xb = x.astype(BF16)
    acc = None
    for s in range(f // FFN_CHUNK):
        lo = s * FFN_CHUNK
        g = _dot(xb, w13_ref[:, lo:lo + FFN_CHUNK])
        u = _dot(xb, w13_ref[:, f + lo:f + lo + FFN_CHUNK])
        h = (g * _sigmoid(g) * u).astype(BF16)
        part = _dot(h, w2_ref[lo:lo + FFN_CHUNK, :])
        acc = part if acc is None else acc + part
    y = _layer_norm(alpha * x + 0.5 * acc, g_ref[...], b_ref[...])
    if with_ple:
        gate = _sigmoid(_dot(y.astype(BF16), wpg_ref[...]))
        y = y + gate * _dot(p_ref[...].astype(BF16), wpe_ref[...])
    o_ref[...] = y
    ob_ref[...] = y.astype(BF16)


def _ffn(x, layer, w13, w2, g, b, alpha, ple=None, tm=512):
    t, d = x.shape
    tm = min(tm, t)
    consts = [w13, w2, g, b]
    in_specs = [pl.BlockSpec((tm, d), lambda i: (i, 0))] + [_layer_spec(a, layer) for a in consts]
    args = [x] + consts
    if ple is not None:
        p, wpg, wpe = ple
        in_specs += [pl.BlockSpec((None, tm, p.shape[2]), lambda i: (layer, i, 0)),
                     _layer_spec(wpg, layer), _layer_spec(wpe, layer)]
        args += [p, wpg, wpe]
    return pl.pallas_call(
        functools.partial(_ffn_kernel, alpha, ple is not None),
        grid=(t // tm,),
        in_specs=in_specs,
        out_specs=[pl.BlockSpec((tm, d), lambda i: (i, 0)),
                   pl.BlockSpec((tm, d), lambda i: (i, 0))],
        out_shape=[jax.ShapeDtypeStruct((t, d), F32), jax.ShapeDtypeStruct((t, d), BF16)],
        compiler_params=_params("parallel"),
        name="ffn_ln_ple" if ple is not None else "ffn_ln",
    )(*args)


def _proj_kernel(tiles_per_seq, hb_ref, w_ref, cw_ref, xg_ref, sb_ref, rqk_ref, rv_ref, xpad_scr):
    tm = hb_ref.shape[0]
    gw = xg_ref.shape[1]

    @pl.when(pl.program_id(0) % tiles_per_seq == 0)
    def _():
        xpad_scr[0:CONV_HALO, :] = jnp.zeros((CONV_HALO, xpad_scr.shape[1]), F32)

    hb = hb_ref[...]
    def conv_chunk(c):
        cs = slice(c * PROJ_CHUNK, (c + 1) * PROJ_CHUNK)
        xpad_scr[CONV_HALO:CONV_HALO + tm, cs] = _dot_nt(hb, w_ref[cs, :])
        acc = cw_ref[GDN_CONV - 1:GDN_CONV, cs] * xpad_scr[CONV_HALO:CONV_HALO + tm, cs]
        for k in range(GDN_CONV - 1):
            off = CONV_HALO - (GDN_CONV - 1) + k
            acc = acc + cw_ref[k:k + 1, cs] * xpad_scr[off:off + tm, cs]
        xpad_scr[0:CONV_HALO, cs] = xpad_scr[tm:tm + CONV_HALO, cs]
        xg_ref[:, cs] = acc * _sigmoid(acc)

    plain = []
    lo = gw
    for ref in (sb_ref, rqk_ref, rv_ref):
        for c in range(ref.shape[1] // PROJ_CHUNK):
            plain.append((ref, c * PROJ_CHUNK, lo + c * PROJ_CHUNK))
        lo += ref.shape[1]
    n_conv = gw // PROJ_CHUNK
    per = -(-len(plain) // n_conv)
    for c in range(n_conv):
        conv_chunk(c)
        for ref, dst, src in plain[c * per:(c + 1) * per]:
            ref[:, dst:dst + PROJ_CHUNK] = _dot_nt(hb, w_ref[src:src + PROJ_CHUNK, :]).astype(ref.dtype)


def _proj(hb, layer, w, conv_w, s, tm=512):
    t, d = hb.shape
    tm = min(tm, s)
    widths = [3 * HW, 3 * HW, 2 * HW, RV_W]
    dts = [F32, BF16, F32, BF16]
    return pl.pallas_call(
        functools.partial(_proj_kernel, s // tm),
        grid=(t // tm,),
        in_specs=[pl.BlockSpec((tm, d), lambda i: (i, 0)), _layer_spec(w, layer), _layer_spec(conv_w, layer)],
        out_specs=[pl.BlockSpec((tm, n), lambda i: (i, 0)) for n in widths],
        out_shape=[jax.ShapeDtypeStruct((t, n), dt) for n, dt in zip(widths, dts)],
        scratch_shapes=[pltpu.VMEM((tm + CONV_HALO, widths[0]), F32)],
        compiler_params=_params("arbitrary"),
        name="mixer_in_proj",
    )(hb, w, conv_w)


def _unit_lower_inverse(lows, ii, jj):
    eye = (ii == jj).astype(F32)
    diag8 = (ii >> 3) == (jj >> 3)
    xs = [jnp.where(diag8, -low, 0.0) for low in lows]
    xbs = [x.astype(BF16) for x in xs]
    x2s = [_dot(xb, xb).astype(BF16) for xb in xbs]
    ts = [eye + x for x in xs]
    ts = [t + _dot(t.astype(BF16), x2) for t, x2 in zip(ts, x2s)]
    x4s = [_dot(x2, x2).astype(BF16) for x2 in x2s]
    ts = [t + _dot(t.astype(BF16), x4) for t, x4 in zip(ts, x4s)]
    shift = 3
    while (1 << shift) < CHUNK:
        pair = ((ii >> (shift + 1)) == (jj >> (shift + 1))) & ((ii >> shift) != (jj >> shift))
        tbs = [t.astype(BF16) for t in ts]
        mids = [_dot(jnp.where(pair, low, 0.0).astype(BF16), tb).astype(BF16) for low, tb in zip(lows, tbs)]
        ts = [t - _dot(tb, mid) for t, tb, mid in zip(ts, tbs, mids)]
        shift += 1
    return ts


def _gdn_kernel(hb_ref, qkv_ref, wb_ref, wa_ref, wat_ref, alog_r_ref, dtb_r_ref,
                alog_c_ref, dtb_c_ref, nw_ref, o_ref, state_scr):
    bsz, tb, _ = qkv_ref.shape

    @pl.when(pl.program_id(0) == 0)
    def _():
        state_scr[...] = jnp.zeros_like(state_scr)

    ii = lax.broadcasted_iota(jnp.int32, (CHUNK, CHUNK), 0)
    jj = lax.broadcasted_iota(jnp.int32, (CHUNK, CHUNK), 1)
    tri_incl = (ii >= jj).astype(BF16)
    tri_incl_t = (ii <= jj).astype(BF16)
    neg_a_r = -jnp.exp(alog_r_ref[...])
    neg_a_c = -jnp.exp(alog_c_ref[...])
    nw = nw_ref[...]
    sq = (CHUNK, CHUNK)
    heads = range(HEADS)

    def l2n(t):
        return t * lax.rsqrt(jnp.sum(t * t, axis=-1, keepdims=True) + 1e-6)

    def local(rows):
        pairs = [(r, h) for r in rows for h in heads]
        cols, cum_rows, s_decs = [], [], []
        for b, r in rows:
            hb_c = hb_ref[b, r, :]
            beta_col = _sigmoid(_dot(hb_c, wb_ref[...]))
            g_col = neg_a_r * _softplus(_dot(hb_c, wa_ref[...]) + dtb_r_ref[...])
            g_row = neg_a_c * _softplus(_dot_nt(wat_ref[...], hb_c) + dtb_c_ref[...])
            gch, gcl = _split(g_col)
            grh, grl = _split(g_row)
            cum_col = _dot(tri_incl, gch) + _dot(tri_incl, gcl)
            cum_rows.append(_dot(grh, tri_incl_t) + _dot(grl, tri_incl_t))
            cum_last = cum_col[CHUNK - 1:CHUNK, :]
            cols.append((beta_col, cum_col, jnp.exp(cum_col), jnp.exp(cum_last - cum_col)))
            s_decs.append(jnp.exp(cum_last))

        n = len(pairs)
        col = lambda j, which, h: jnp.broadcast_to(cols[j][which][:, h:h + 1], sq)
        chunk_of = [j for j in range(len(rows)) for _ in heads]
        head_of = [h for _ in rows for h in heads]
        q = [l2n(qkv_ref[b, r, h * HEAD_DIM:(h + 1) * HEAD_DIM]) * (HEAD_DIM ** -0.5) for (b, r), h in pairs]
        k = [l2n(qkv_ref[b, r, HW + h * HEAD_DIM:HW + (h + 1) * HEAD_DIM]) for (b, r), h in pairs]
        v = [qkv_ref[b, r, 2 * HW + h * HEAD_DIM:2 * HW + (h + 1) * HEAD_DIM] for (b, r), h in pairs]
        beta = [col(chunk_of[p], 0, head_of[p]) for p in range(n)]
        e_g = [col(chunk_of[p], 2, head_of[p]) for p in range(n)]
        dec = [jnp.exp(jnp.where(ii >= jj, col(chunk_of[p], 1, head_of[p])
                                 - jnp.broadcast_to(cum_rows[chunk_of[p]][head_of[p]:head_of[p] + 1, :], sq), -1e30))
               for p in range(n)]
        kb = [k[p] * beta[p] for p in range(n)]
        kbf = [k[p].astype(BF16) for p in range(n)]
        kq = [_dot_nt(jnp.concatenate([kb[p], q[p]], axis=0).astype(BF16), kbf[p]) for p in range(n)]
        low = [jnp.where(ii > jj, kq[p][:CHUNK] * dec[p], 0.0) for p in range(n)]
        t_inv = _unit_lower_inverse(low, ii, jj)
        rhs = [jnp.concatenate([v[p] * beta[p], kb[p] * e_g[p]], axis=1).astype(BF16) for p in range(n)]
        sol = [_dot(t_inv[p].astype(BF16), rhs[p]) for p in range(n)]
        a_qk = [(kq[p][CHUNK:] * dec[p]).astype(BF16) for p in range(n)]
        wq = [jnp.concatenate([sol[p][:, HEAD_DIM:], q[p] * e_g[p]], axis=0).astype(BF16) for p in range(n)]
        u = [sol[p][:, :HEAD_DIM] for p in range(n)]
        k_dec = [(k[p] * col(chunk_of[p], 3, head_of[p])).astype(BF16) for p in range(n)]
        return u, wq, a_qk, k_dec, s_decs

    def recurrent(rows, c0, u, wq, a_qk, k_dec, s_decs):
        chains = [(i, h) for i in range(len(rows)) for h in heads]
        state = [state_scr[rows[i][0] * HEADS + h] for i, h in chains]
        pair = [(c0 + i) * HEADS + h for i, h in chains]
        from_state = [_dot(wq[p], s.astype(BF16)) for p, s in zip(pair, state)]
        v_new = [(u[p] - fs[:CHUNK]).astype(BF16) for p, fs in zip(pair, from_state)]
        o = [fs[CHUNK:] + _dot(a_qk[p], vn) for p, fs, vn in zip(pair, from_state, v_new)]
        for (i, h), p, s, vn in zip(chains, pair, state, v_new):
            state_scr[rows[i][0] * HEADS + h] = s * s_decs[c0 + i][:, h:h + 1] + _dot_tn(k_dec[p], vn)
        for (i, h), on in zip(chains, o):
            on = on * lax.rsqrt(jnp.mean(on * on, axis=-1, keepdims=True) + NORM_EPS) * nw
            o_ref[rows[i][0], rows[i][1], h * HEAD_DIM:(h + 1) * HEAD_DIM] = on

    def chunk_group(c, carry):
        slices = [pl.ds(pl.multiple_of((c * GDN_GROUP + j) * CHUNK, CHUNK), CHUNK) for j in range(GDN_GROUP)]
        rows = [(b, r) for r in slices for b in range(bsz)]
        parts = local(rows)
        for j in range(GDN_GROUP):
            recurrent(rows[j * bsz:(j + 1) * bsz], j * bsz, *parts)
        return carry

    lax.fori_loop(0, tb // (CHUNK * GDN_GROUP), chunk_group, 0)


def _gdn(hb, xg, layer, smalls, tb=512):
    bsz, s, d = hb.shape
    tb = min(tb, s)
    return pl.pallas_call(
        _gdn_kernel,
        grid=(s // tb,),
        in_specs=[pl.BlockSpec((bsz, tb, d), lambda i: (0, i, 0)),
                  pl.BlockSpec((bsz, tb, 3 * HW), lambda i: (0, i, 0))] + [_layer_spec(a, layer) for a in smalls],
        out_specs=pl.BlockSpec((bsz, tb, HW), lambda i: (0, i, 0)),
        out_shape=jax.ShapeDtypeStruct((bsz, s, HW), F32),
        scratch_shapes=[pltpu.VMEM((bsz * HEADS, HEAD_DIM, HEAD_DIM), F32)],
        compiler_params=_params("arbitrary"),
        name="gated_deltanet",
    )(hb, xg, *smalls)


def _sb_kernel(q_ref, k_ref, v_ref, o_ref, out_scr, spent_scr):
    tq = q_ref.shape[1]
    tk = tq
    qi = pl.program_id(1)
    heads = range(HEADS)
    hs = lambda h: slice(h * HEAD_DIM, (h + 1) * HEAD_DIM)
    row = lax.broadcasted_iota(jnp.int32, (tq, tk), 0)
    col = lax.broadcasted_iota(jnp.int32, (tq, tk), 1)
    suffix = (row >= col).astype(BF16)
    suffix2 = jnp.concatenate([suffix, suffix], axis=0)
    q = [q_ref[0, :, hs(h)] for h in heads]
    wide = lambda a: jnp.concatenate([a] * (tk // HEAD_DIM), axis=1)

    def block(kj, spent, valid):
        keys = pl.ds(pl.multiple_of(kj * tk, tk), tk)
        z = [_dot_nt(q[h], k_ref[0, keys, hs(h)]) for h in heads]
        sp = [_softplus(z[h]) for h in heads]
        nlk = sp if valid is None else [jnp.where(valid, sp[h], 0.0) for h in heads]
        sums = [_dot(jnp.concatenate(_split(nlk[h]), axis=1), suffix2) for h in heads]
        wgt = [jnp.exp(z[h] - sums[h] - wide(spent[h])) for h in heads]
        if valid is not None:
            wgt = [jnp.where(valid, wgt[h], 0.0) for h in heads]
        pv = [_dot(wgt[h].astype(BF16), v_ref[0, keys, hs(h)]) for h in heads]
        spent = [spent[h] + jnp.broadcast_to(sums[h][:, 0:1], (tq, HEAD_DIM)) for h in heads]
        least = jnp.minimum(jnp.minimum(spent[0], spent[1]), jnp.minimum(spent[2], spent[3]))
        return pv, spent, jnp.min(least)

    zero = jnp.zeros((tq, HEAD_DIM), F32)
    pv, spent, least = block(qi, [zero] * HEADS, col < row)
    for h in heads:
        out_scr[:, hs(h)] = pv[h]
        spent_scr[h] = spent[h]

    def cond(carry):
        kj, spent_min = carry
        return jnp.logical_and(kj >= 0, spent_min < SB_STICK_EXHAUSTED)

    def body(carry):
        kj, _ = carry
        pv, spent, least = block(kj, [spent_scr[h] for h in heads], None)
        for h in heads:
            out_scr[:, hs(h)] += pv[h]
            spent_scr[h] = spent[h]
        return kj - 1, least

    lax.while_loop(cond, body, (qi - 1, least))
    o_ref[0] = out_scr[...].astype(o_ref.dtype)


def _stick_breaking(qkv, tq=256):
    bsz, s, _ = qkv.shape
    tq = min(tq, s)
    resident = lambda j: pl.BlockSpec((1, s, HW), lambda b, i: (b, 0, j), pipeline_mode=pl.Buffered(1))
    return pl.pallas_call(
        _sb_kernel,
        grid=(bsz, s // tq),
        in_specs=[pl.BlockSpec((1, tq, HW), lambda b, i: (b, i, 0)), resident(1), resident(2)],
        out_specs=pl.BlockSpec((1, tq, HW), lambda b, i: (b, i, 0)),
        out_shape=jax.ShapeDtypeStruct((bsz, s, HW), BF16),
        scratch_shapes=[pltpu.VMEM((tq, HW), F32), pltpu.VMEM((HEADS, tq, HEAD_DIM), F32)],
        compiler_params=_params("parallel", "arbitrary"),
        name="stick_breaking",
    )(qkv, qkv, qkv)


def _merge_ret_kernel(alpha, tiles_per_seq, x_ref, hb_ref, oa_ref, ob_ref, qk_ref, v_ref, cos_ref, sin_ref,
                      rnw_ref, wg_ref, wbr_ref, wo_ref, g_ref, b_ref, o_ref, obf_ref,
                      state_scr, oc_scr, part_scr, gate_scr):
    tm = x_ref.shape[0]
    d = wo_ref.shape[0]
    heads = range(HEADS)
    o_ma = HW + RV_W

    @pl.when(pl.program_id(0) % tiles_per_seq == 0)
    def _():
        state_scr[...] = jnp.zeros_like(state_scr)

    ii = lax.broadcasted_iota(jnp.int32, (CHUNK, CHUNK), 0)
    jj = lax.broadcasted_iota(jnp.int32, (CHUNK, CHUNK), 1)
    rel = (ii - jj).astype(F32)
    pos = ii.astype(F32)
    rnw = rnw_ref[...]
    log_gamma = [math.log(1.0 - 2.0 ** (-5.0 - h)) for h in heads]
    dmat = [jnp.where(rel >= 0, jnp.exp(jnp.maximum(rel, 0.0) * lg), 0.0) for lg in log_gamma]
    q_dec = [jnp.exp((pos + 1.0) * lg) for lg in log_gamma]
    k_dec = [jnp.exp((CHUNK - 1.0 - pos) * lg) * (HEAD_DIM ** -0.5) for lg in log_gamma]

    def rot(t, cos, sin):
        return t * cos + pltpu.roll(t, HEAD_DIM // 2, 1) * sin

    def silu(t):
        return t * _sigmoid(t)

    def ret_chunk(c):
        rows = slice(c * CHUNK, (c + 1) * CHUNK)
        cos = cos_ref[rows, :]
        sin = sin_ref[rows, :]
        q = [rot(qk_ref[rows, h * HEAD_DIM:(h + 1) * HEAD_DIM], cos, sin) for h in heads]
        k = [rot(qk_ref[rows, HW + h * HEAD_DIM:HW + (h + 1) * HEAD_DIM], cos, sin) for h in heads]
        v = [v_ref[rows, h * RET_DV:(h + 1) * RET_DV] for h in heads]
        a = [(_dot_nt(q[h].astype(BF16), (k[h] * (HEAD_DIM ** -0.5)).astype(BF16)) * dmat[h]).astype(BF16)
             for h in heads]
        state = [state_scr[h] for h in heads]
        o = [_dot(a[h], v[h]) + _dot((q[h] * q_dec[h]).astype(BF16), state[h].astype(BF16)) for h in heads]
        for h in heads:
            state_scr[h] = state[h] * math.exp(CHUNK * log_gamma[h]) + _dot_tn((k[h] * k_dec[h]).astype(BF16), v[h])
        for h in heads:
            mu = jnp.mean(o[h], axis=-1, keepdims=True)
            dev = o[h] - mu
            var = jnp.mean(dev * dev, axis=-1, keepdims=True)
            oc_scr[rows, h * RET_DV:(h + 1) * RET_DV] = dev * lax.rsqrt(var + NORM_EPS) * rnw

    hb = hb_ref[...]
    oa = (oa_ref[...] * silu(_dot_nt(hb, wg_ref[0:HW, :]))).astype(BF16)
    ob = ob_ref[...]
    gate = lambda j, lo: _sigmoid(_dot_nt(hb, wg_ref[o_ma + j * d + lo:o_ma + j * d + lo + MERGE_CHUNK, :]))
    n_cols = d // MERGE_CHUNK
    n_ret = tm // CHUNK
    for c in range(max(n_cols, n_ret)):
        if c < n_cols:
            lo = c * MERGE_CHUNK
            cs = slice(lo, lo + MERGE_CHUNK)
            part_scr[:, cs] = (gate(0, lo) * _dot(oa, wbr_ref[0:HW, cs])
                               + gate(1, lo) * _dot(ob, wbr_ref[HW:2 * HW, cs]))
            gate_scr[:, cs] = gate(2, lo)
        if c < n_ret:
            ret_chunk(c)
    oc = (oc_scr[...] * silu(_dot_nt(hb, wg_ref[HW:o_ma, :]))).astype(BF16)
    mix = None
    for c in range(n_cols):
        lo = c * MERGE_CHUNK
        cs = slice(lo, lo + MERGE_CHUNK)
        merged = part_scr[:, cs] + gate_scr[:, cs] * _dot(oc, wbr_ref[2 * HW:2 * HW + RV_W, cs])
        part = _dot(merged.astype(BF16), wo_ref[cs, :])
        mix = part if mix is None else mix + part
    y = _layer_norm(alpha * x_ref[...] + mix, g_ref[...], b_ref[...])
    o_ref[...] = y
    obf_ref[...] = y.astype(BF16)


def _merge_ret(x, hb, oa, ob, rqk, rv, cos, sin, layer, consts, alpha, s, tm=512):
    t, d = x.shape
    tm = min(tm, s)
    tiles_per_seq = s // tm
    acts = [x, hb, oa, ob, rqk, rv]
    table = pl.BlockSpec((tm, HEAD_DIM), lambda i: (i % tiles_per_seq, 0))
    return pl.pallas_call(
        functools.partial(_merge_ret_kernel, alpha, tiles_per_seq),
        grid=(t // tm,),
        in_specs=[pl.BlockSpec((tm, a.shape[1]), lambda i: (i, 0)) for a in acts] + [table, table]
        + [_layer_spec(w, layer) for w in consts],
        out_specs=[pl.BlockSpec((tm, d), lambda i: (i, 0)), pl.BlockSpec((tm, d), lambda i: (i, 0))],
        out_shape=[jax.ShapeDtypeStruct((t, d), F32), jax.ShapeDtypeStruct((t, d), BF16)],
        scratch_shapes=[pltpu.VMEM((HEADS, HEAD_DIM, RET_DV), F32), pltpu.VMEM((tm, RV_W), F32),
                        pltpu.VMEM((tm, d), F32), pltpu.VMEM((tm, d), F32)],
        compiler_params=_params("arbitrary"),
        name="retention_merge_ln",
    )(*acts, cos, sin, *consts)


def _rotary_tables(s):
    half = HEAD_DIM // 2
    inv = ROPE_BASE ** (-jnp.arange(half, dtype=F32) / half)
    ang = jnp.arange(s).astype(F32)[:, None] * inv[None, :]
    cos, sin = jnp.cos(ang), jnp.sin(ang)
    return jnp.concatenate([cos, cos], axis=-1), jnp.concatenate([-sin, sin], axis=-1)


def kernel(x, p, ffn1_w13, ffn1_w2, w_in, gdn_conv_w, gdn_a_log, gdn_dt_bias, gdn_norm_w, ret_norm_w, w_br_gdn, w_br_sb, w_br_ret, w_out, ffn2_w13, ffn2_w2, ln_g, ln_b, w_ple_gate, w_ple_proj):
    bsz, s, d = x.shape
    depth = ffn1_w13.shape[0]
    t = bsz * s
    alpha = (2 * depth) ** 0.25
    bf = lambda a: a.astype(BF16)
    w_in_t = jnp.swapaxes(w_in, 1, 2)
    o_sq = 4 * HW + 2 * HEADS
    o_rg = o_sq + 5 * HW + RV_W
    w_proj = bf(jnp.concatenate([w_in_t[:, 0:3 * HW], w_in_t[:, o_sq:o_sq + HW] * (HEAD_DIM ** -0.5),
                                 w_in_t[:, o_sq + HW:o_rg]], axis=1))
    w_gate = bf(jnp.concatenate([w_in_t[:, 3 * HW:4 * HW], w_in_t[:, o_rg:]], axis=1))
    w_branch = bf(jnp.concatenate([w_br_gdn, w_br_sb, w_br_ret], axis=1))
    wb_t = bf(w_in_t[:, 4 * HW:4 * HW + HEADS])
    wa_t = bf(w_in_t[:, 4 * HW + HEADS:4 * HW + 2 * HEADS])
    gdn_smalls = [jnp.swapaxes(wb_t, 1, 2), jnp.swapaxes(wa_t, 1, 2), wa_t,
                  gdn_a_log.reshape(depth, 1, HEADS), gdn_dt_bias.reshape(depth, 1, HEADS),
                  gdn_a_log.reshape(depth, HEADS, 1), gdn_dt_bias.reshape(depth, HEADS, 1),
                  gdn_norm_w.reshape(depth, 1, HEAD_DIM)]
    ret_nw = ret_norm_w.reshape(depth, 1, RET_DV)
    ffn1 = (bf(ffn1_w13), bf(ffn1_w2))
    ffn2 = (bf(ffn2_w13), bf(ffn2_w2))
    w_o, w_pg, w_pe = bf(w_out), bf(w_ple_gate), bf(w_ple_proj)
    ln = lambda a, j: a[:, j].reshape(depth, 1, d)
    p2 = p.reshape(depth, t, p.shape[-1])
    cos, sin = _rotary_tables(s)

    xf = x.reshape(t, d)
    for i in range(depth):
        xf, xb = _ffn(xf, i, *ffn1, ln(ln_g, 0), ln(ln_b, 0), alpha)
        xg, sqkv, rqk, rv = _proj(xb, i, w_proj, gdn_conv_w, s)
        oa = _gdn(xb.reshape(bsz, s, d), xg.reshape(bsz, s, 3 * HW), i, gdn_smalls)
        ob = _stick_breaking(sqkv.reshape(bsz, s, 3 * HW))
        xf, xb = _merge_ret(xf, xb, oa.reshape(t, HW), ob.reshape(t, HW), rqk, rv, cos, sin, i,
                            [ret_nw, w_gate, w_branch, w_o, ln(ln_g, 1), ln(ln_b, 1)], alpha, s)
        xf, xb = _ffn(xf, i, *ffn2, ln(ln_g, 2), ln(ln_b, 2), alpha, ple=(p2, w_pg, w_pe))
    return xf.reshape(bsz, s, d)
```

```python
import functools
import math

import jax
import jax.numpy as jnp
from jax import lax
from jax.experimental import pallas as pl
from jax.experimental.pallas import tpu as pltpu

F32 = jnp.float32
BF16 = jnp.bfloat16

NORM_EPS = 1e-5
HEADS = 4
HEAD_DIM = 128
RET_DV = 256
GDN_CONV = 4
ROPE_BASE = 10000.0
CHUNK = 128
CONV_HALO = 8
GDN_GROUP = 2
MERGE_CHUNK = 256
FFN_CHUNK = 256
PROJ_CHUNK = 256
SB_STICK_EXHAUSTED = 110.0
VMEM_LIMIT = 56 * 1024 * 1024

HW = HEADS * HEAD_DIM
RV_W = HEADS * RET_DV


def _dot(a, b):
    return jnp.dot(a, b, preferred_element_type=F32)


def _dot_nt(a, b):
    return lax.dot_general(a, b, (((1,), (1,)), ((), ())), preferred_element_type=F32)


def _dot_tn(a, b):
    return lax.dot_general(a, b, (((0,), (0,)), ((), ())), preferred_element_type=F32)


def _split(a):
    hi = a.astype(BF16)
    lo = (a - hi.astype(F32)).astype(BF16)
    return hi, lo


def _softplus(x):
    return jnp.maximum(x, 0.0) + jnp.log(1.0 + jnp.exp(-jnp.abs(x)))


def _sigmoid(x):
    return 1.0 / (1.0 + jnp.exp(-x))


def _layer_norm(y, g, b):
    mu = jnp.mean(y, axis=-1, keepdims=True)
    d = y - mu
    var = jnp.mean(d * d, axis=-1, keepdims=True)
    return d * lax.rsqrt(var + NORM_EPS) * g + b


def _params(*sem):
    return pltpu.CompilerParams(dimension_semantics=sem, vmem_limit_bytes=VMEM_LIMIT)


def _layer_spec(a, layer):
    zeros = (0,) * (a.ndim - 1)
    return pl.BlockSpec((None,) + a.shape[1:], lambda *_: (layer,) + zeros, pipeline_mode=pl.Buffered(1))


def _ffn_kernel(alpha, with_ple, x_ref, w13_ref, w2_ref, g_ref, b_ref, *rest):
    if with_ple:
        p_ref, wpg_ref, wpe_ref, o_ref, ob_ref = rest
    else:
        o_ref, ob_ref = rest
    f = w2_ref.shape[0]
    x = x_ref[...]
    xb = x.astype(BF16)
    acc = None
    for s in range(f // FFN_CHUNK):
        lo = s * FFN_CHUNK
        g = _dot(xb, w13_ref[:, lo:lo + FFN_CHUNK])
        u = _dot(xb, w13_ref[:, f + lo:f + lo + FFN_CHUNK])
        h = (g * _sigmoid(g) * u).astype(BF16)
        part = _dot(h, w2_ref[lo:lo + FFN_CHUNK, :])
        acc = part if acc is None else acc + part
    y = _layer_norm(alpha * x + 0.5 * acc, g_ref[...], b_ref[...])
    if with_ple:
        gate = _sigmoid(_dot(y.astype(BF16), wpg_ref[...]))
        y = y + gate * _dot(p_ref[...].astype(BF16), wpe_ref[...])
    o_ref[...] = y
    ob_ref[...] = y.astype(BF16)


def _ffn(x, layer, w13, w2, g, b, alpha, ple=None, tm=512):
    t, d = x.shape
    tm = min(tm, t)
    consts = [w13, w2, g, b]
    in_specs = [pl.BlockSpec((tm, d), lambda i: (i, 0))] + [_layer_spec(a, layer) for a in consts]
    args = [x] + consts
    if ple is not None:
        p, wpg, wpe = ple
        in_specs += [pl.BlockSpec((None, tm, p.shape[2]), lambda i: (layer, i, 0)),
                     _layer_spec(wpg, layer), _layer_spec(wpe, layer)]
        args += [p, wpg, wpe]
    return pl.pallas_call(
        functools.partial(_ffn_kernel, alpha, ple is not None),
        grid=(t // tm,),
        in_specs=in_specs,
        out_specs=[pl.BlockSpec((tm, d), lambda i: (i, 0)),
                   pl.BlockSpec((tm, d), lambda i: (i, 0))],
        out_shape=[jax.ShapeDtypeStruct((t, d), F32), jax.ShapeDtypeStruct((t, d), BF16)],
        compiler_params=_params("parallel"),
        name="ffn_ln_ple" if ple is not None else "ffn_ln",
    )(*args)


def _proj_kernel(tiles_per_seq, hb_ref, w_ref, cw_ref, xg_ref, sb_ref, rqk_ref, rv_ref, xpad_scr):
    tm = hb_ref.shape[0]
    gw = xg_ref.shape[1]

    @pl.when(pl.program_id(0) % tiles_per_seq == 0)
    def _():
        xpad_scr[0:CONV_HALO, :] = jnp.zeros((CONV_HALO, xpad_scr.shape[1]), F32)

    hb = hb_ref[...]
    def conv_chunk(c):
        cs = slice(c * PROJ_CHUNK, (c + 1) * PROJ_CHUNK)
        xpad_scr[CONV_HALO:CONV_HALO + tm, cs] = _dot(hb, w_ref[:, cs])
        acc = cw_ref[GDN_CONV - 1:GDN_CONV, cs] * xpad_scr[CONV_HALO:CONV_HALO + tm, cs]
        for k in range(GDN_CONV - 1):
            off = CONV_HALO - (GDN_CONV - 1) + k
            acc = acc + cw_ref[k:k + 1, cs] * xpad_scr[off:off + tm, cs]
        xpad_scr[0:CONV_HALO, cs] = xpad_scr[tm:tm + CONV_HALO, cs]
        xg_ref[:, cs] = acc * _sigmoid(acc)

    plain = []
    lo = gw
    for ref in (sb_ref, rqk_ref, rv_ref):
        for c in range(ref.shape[1] // PROJ_CHUNK):
            plain.append((ref, c * PROJ_CHUNK, lo + c * PROJ_CHUNK))
        lo += ref.shape[1]
    n_conv = gw // PROJ_CHUNK
    per = -(-len(plain) // n_conv)
    for c in range(n_conv):
        conv_chunk(c)
        for ref, dst, src in plain[c * per:(c + 1) * per]:
            ref[:, dst:dst + PROJ_CHUNK] = _dot(hb, w_ref[:, src:src + PROJ_CHUNK]).astype(ref.dtype)


def _proj(hb, layer, w, conv_w, s, tm=512):
    t, d = hb.shape
    tm = min(tm, s)
    widths = [3 * HW, 3 * HW, 2 * HW, RV_W]
    dts = [F32, BF16, F32, BF16]
    return pl.pallas_call(
        functools.partial(_proj_kernel, s // tm),
        grid=(t // tm,),
        in_specs=[pl.BlockSpec((tm, d), lambda i: (i, 0)), _layer_spec(w, layer), _layer_spec(conv_w, layer)],
        out_specs=[pl.BlockSpec((tm, n), lambda i: (i, 0)) for n in widths],
        out_shape=[jax.ShapeDtypeStruct((t, n), dt) for n, dt in zip(widths, dts)],
        scratch_shapes=[pltpu.VMEM((tm + CONV_HALO, widths[0]), F32)],
        compiler_params=_params("arbitrary"),
        name="mixer_in_proj",
    )(hb, w, conv_w)


def _unit_lower_inverse(lows, ii, jj):
    eye = (ii == jj).astype(F32)
    diag8 = (ii >> 3) == (jj >> 3)
    xs = [jnp.where(diag8, -low, 0.0) for low in lows]
    xbs = [x.astype(BF16) for x in xs]
    x2s = [_dot(xb, xb).astype(BF16) for xb in xbs]
    ts = [eye + x for x in xs]
    ts = [t + _dot(t.astype(BF16), x2) for t, x2 in zip(ts, x2s)]
    x4s = [_dot(x2, x2).astype(BF16) for x2 in x2s]
    ts = [t + _dot(t.astype(BF16), x4) for t, x4 in zip(ts, x4s)]
    shift = 3
    while (1 << shift) < CHUNK:
        pair = ((ii >> (shift + 1)) == (jj >> (shift + 1))) & ((ii >> shift) != (jj >> shift))
        tbs = [t.astype(BF16) for t in ts]
        mids = [_dot(jnp.where(pair, low, 0.0).astype(BF16), tb).astype(BF16) for low, tb in zip(lows, tbs)]
        ts = [t - _dot(tb, mid) for t, tb, mid in zip(ts, tbs, mids)]
        shift += 1
    return ts


def _gdn_kernel(hb_ref, qkv_ref, wb_ref, wa_ref, wat_ref, alog_r_ref, dtb_r_ref,
                alog_c_ref, dtb_c_ref, nw_ref, o_ref, state_scr):
    bsz, tb, _ = qkv_ref.shape

    @pl.when(pl.program_id(0) == 0)
    def _():
        state_scr[...] = jnp.zeros_like(state_scr)

    ii = lax.broadcasted_iota(jnp.int32, (CHUNK, CHUNK), 0)
    jj = lax.broadcasted_iota(jnp.int32, (CHUNK, CHUNK), 1)
    tri_incl = (ii >= jj).astype(BF16)
    tri_incl_t = (ii <= jj).astype(BF16)
    neg_a_r = -jnp.exp(alog_r_ref[...])
    neg_a_c = -jnp.exp(alog_c_ref[...])
    nw = nw_ref[...]
    sq = (CHUNK, CHUNK)
    heads = range(HEADS)

    def l2n(t):
        return t * lax.rsqrt(jnp.sum(t * t, axis=-1, keepdims=True) + 1e-6)

    def local(rows):
        pairs = [(r, h) for r in rows for h in heads]
        cols, cum_rows, s_decs = [], [], []
        for b, r in rows:
            hb_c = hb_ref[b, r, :]
            beta_col = _sigmoid(_dot(hb_c, wb_ref[...]))
            g_col = neg_a_r * _softplus(_dot(hb_c, wa_ref[...]) + dtb_r_ref[...])
            g_row = neg_a_c * _softplus(_dot_nt(wat_ref[...], hb_c) + dtb_c_ref[...])
            gch, gcl = _split(g_col)
            grh, grl = _split(g_row)
            cum_col = _dot(tri_incl, gch) + _dot(tri_incl, gcl)
            cum_rows.append(_dot(grh, tri_incl_t) + _dot(grl, tri_incl_t))
            cum_last = cum_col[CHUNK - 1:CHUNK, :]
            cols.append((beta_col, cum_col, jnp.exp(cum_col), jnp.exp(cum_last - cum_col)))
            s_decs.append(jnp.exp(cum_last))

        n = len(pairs)
        col = lambda j, which, h: jnp.broadcast_to(cols[j][which][:, h:h + 1], sq)
        chunk_of = [j for j in range(len(rows)) for _ in heads]
        head_of = [h for _ in rows for h in heads]
        q = [l2n(qkv_ref[b, r, h * HEAD_DIM:(h + 1) * HEAD_DIM]) * (HEAD_DIM ** -0.5) for (b, r), h in pairs]
        k = [l2n(qkv_ref[b, r, HW + h * HEAD_DIM:HW + (h + 1) * HEAD_DIM]) for (b, r), h in pairs]
        v = [qkv_ref[b, r, 2 * HW + h * HEAD_DIM:2 * HW + (h + 1) * HEAD_DIM] for (b, r), h in pairs]
        beta = [col(chunk_of[p], 0, head_of[p]) for p in range(n)]
        e_g = [col(chunk_of[p], 2, head_of[p]) for p in range(n)]
        dec = [jnp.exp(jnp.where(ii >= jj, col(chunk_of[p], 1, head_of[p])
                                 - jnp.broadcast_to(cum_rows[chunk_of[p]][head_of[p]:head_of[p] + 1, :], sq), -1e30))
               for p in range(n)]
        kb = [k[p] * beta[p] for p in range(n)]
        kbf = [k[p].astype(BF16) for p in range(n)]
        kq = [_dot_nt(jnp.concatenate([kb[p], q[p]], axis=0).astype(BF16), kbf[p]) for p in range(n)]
        low = [jnp.where(ii > jj, kq[p][:CHUNK] * dec[p], 0.0) for p in range(n)]
        t_inv = _unit_lower_inverse(low, ii, jj)
        rhs = [jnp.concatenate([v[p] * beta[p], kb[p] * e_g[p]], axis=1).astype(BF16) for p in range(n)]
        sol = [_dot(t_inv[p].astype(BF16), rhs[p]) for p in range(n)]
        a_qk = [(kq[p][CHUNK:] * dec[p]).astype(BF16) for p in range(n)]
        wq = [jnp.concatenate([sol[p][:, HEAD_DIM:], q[p] * e_g[p]], axis=0).astype(BF16) for p in range(n)]
        u = [sol[p][:, :HEAD_DIM] for p in range(n)]
        k_dec = [(k[p] * col(chunk_of[p], 3, head_of[p])).astype(BF16) for p in range(n)]
        return u, wq, a_qk, k_dec, s_decs

    def recurrent(rows, c0, u, wq, a_qk, k_dec, s_decs):
        chains = [(i, h) for i in range(len(rows)) for h in heads]
        state = [state_scr[rows[i][0] * HEADS + h] for i, h in chains]
        pair = [(c0 + i) * HEADS + h for i, h in chains]
        from_state = [_dot(wq[p], s.astype(BF16)) for p, s in zip(pair, state)]
        v_new = [(u[p] - fs[:CHUNK]).astype(BF16) for p, fs in zip(pair, from_state)]
        o = [fs[CHUNK:] + _dot(a_qk[p], vn) for p, fs, vn in zip(pair, from_state, v_new)]
        for (i, h), p, s, vn in zip(chains, pair, state, v_new):
            state_scr[rows[i][0] * HEADS + h] = s * s_decs[c0 + i][:, h:h + 1] + _dot_tn(k_dec[p], vn)
        for (i, h), on in zip(chains, o):
            on = on * lax.rsqrt(jnp.mean(on * on, axis=-1, keepdims=True) + NORM_EPS) * nw
            o_ref[rows[i][0], rows[i][1], h * HEAD_DIM:(h + 1) * HEAD_DIM] = on

    def chunk_group(c, carry):
        slices = [pl.ds(pl.multiple_of((c * GDN_GROUP + j) * CHUNK, CHUNK), CHUNK) for j in range(GDN_GROUP)]
        rows = [(b, r) for r in slices for b in range(bsz)]
        parts = local(rows)
        for j in range(GDN_GROUP):
            recurrent(rows[j * bsz:(j + 1) * bsz], j * bsz, *parts)
        return carry

    lax.fori_loop(0, tb // (CHUNK * GDN_GROUP), chunk_group, 0)


def _gdn(hb, xg, layer, smalls, tb=512):
    bsz, s, d = hb.shape
    tb = min(tb, s)
    return pl.pallas_call(
        _gdn_kernel,
        grid=(s // tb,),
        in_specs=[pl.BlockSpec((bsz, tb, d), lambda i: (0, i, 0)),
                  pl.BlockSpec((bsz, tb, 3 * HW), lambda i: (0, i, 0))] + [_layer_spec(a, layer) for a in smalls],
        out_specs=pl.BlockSpec((bsz, tb, HW), lambda i: (0, i, 0)),
        out_shape=jax.ShapeDtypeStruct((bsz, s, HW), F32),
        scratch_shapes=[pltpu.VMEM((bsz * HEADS, HEAD_DIM, HEAD_DIM), F32)],
        compiler_params=_params("arbitrary"),
        name="gated_deltanet",
    )(hb, xg, *smalls)


def _sb_kernel(q_ref, k_ref, v_ref, o_ref, out_scr, spent_scr):
    tq = q_ref.shape[1]
    tk = tq
    qi = pl.program_id(1)
    heads = range(HEADS)
    hs = lambda h: slice(h * HEAD_DIM, (h + 1) * HEAD_DIM)
    row = lax.broadcasted_iota(jnp.int32, (tq, tk), 0)
    col = lax.broadcasted_iota(jnp.int32, (tq, tk), 1)
    suffix = (row >= col).astype(BF16)
    suffix2 = jnp.concatenate([suffix, suffix], axis=0)
    q = [q_ref[0, :, hs(h)] for h in heads]
    wide = lambda a: jnp.concatenate([a] * (tk // HEAD_DIM), axis=1)

    def block(kj, spent, valid):
        keys = pl.ds(pl.multiple_of(kj * tk, tk), tk)
        z = [_dot_nt(q[h], k_ref[0, keys, hs(h)]) for h in heads]
        sp = [_softplus(z[h]) for h in heads]
        nlk = sp if valid is None else [jnp.where(valid, sp[h], 0.0) for h in heads]
        sums = [_dot(jnp.concatenate(_split(nlk[h]), axis=1), suffix2) for h in heads]
        wgt = [jnp.exp(z[h] - sums[h] - wide(spent[h])) for h in heads]
        if valid is not None:
            wgt = [jnp.where(valid, wgt[h], 0.0) for h in heads]
        pv = [_dot(wgt[h].astype(BF16), v_ref[0, keys, hs(h)]) for h in heads]
        spent = [spent[h] + jnp.broadcast_to(sums[h][:, 0:1], (tq, HEAD_DIM)) for h in heads]
        least = jnp.minimum(jnp.minimum(spent[0], spent[1]), jnp.minimum(spent[2], spent[3]))
        return pv, spent, jnp.min(least)

    zero = jnp.zeros((tq, HEAD_DIM), F32)
    pv, spent, least = block(qi, [zero] * HEADS, col < row)
    for h in heads:
        out_scr[:, hs(h)] = pv[h]
        spent_scr[h] = spent[h]

    def cond(carry):
        kj, spent_min = carry
        return jnp.logical_and(kj >= 0, spent_min < SB_STICK_EXHAUSTED)

    def body(carry):
        kj, _ = carry
        pv, spent, least = block(kj, [spent_scr[h] for h in heads], None)
        for h in heads:
            out_scr[:, hs(h)] += pv[h]
            spent_scr[h] = spent[h]
        return kj - 1, least

    lax.while_loop(cond, body, (qi - 1, least))
    o_ref[0] = out_scr[...].astype(o_ref.dtype)


def _stick_breaking(qkv, tq=256):
    bsz, s, _ = qkv.shape
    tq = min(tq, s)
    resident = lambda j: pl.BlockSpec((1, s, HW), lambda b, i: (b, 0, j), pipeline_mode=pl.Buffered(1))
    return pl.pallas_call(
        _sb_kernel,
        grid=(bsz, s // tq),
        in_specs=[pl.BlockSpec((1, tq, HW), lambda b, i: (b, i, 0)), resident(1), resident(2)],
        out_specs=pl.BlockSpec((1, tq, HW), lambda b, i: (b, i, 0)),
        out_shape=jax.ShapeDtypeStruct((bsz, s, HW), BF16),
        scratch_shapes=[pltpu.VMEM((tq, HW), F32), pltpu.VMEM((HEADS, tq, HEAD_DIM), F32)],
        compiler_params=_params("parallel", "arbitrary"),
        name="stick_breaking",
    )(qkv, qkv, qkv)


def _merge_ret_kernel(alpha, tiles_per_seq, x_ref, hb_ref, oa_ref, ob_ref, qk_ref, v_ref, cos_ref, sin_ref,
                      rnw_ref, wg_ref, wbr_ref, wo_ref, g_ref, b_ref, o_ref, obf_ref,
                      state_scr, oc_scr, part_scr, gate_scr):
    tm = x_ref.shape[0]
    d = wo_ref.shape[0]
    heads = range(HEADS)
    o_ma = HW + RV_W

    @pl.when(pl.program_id(0) % tiles_per_seq == 0)
    def _():
        state_scr[...] = jnp.zeros_like(state_scr)

    ii = lax.broadcasted_iota(jnp.int32, (CHUNK, CHUNK), 0)
    jj = lax.broadcasted_iota(jnp.int32, (CHUNK, CHUNK), 1)
    rel = (ii - jj).astype(F32)
    pos = ii.astype(F32)
    rnw = rnw_ref[...]
    log_gamma = [math.log(1.0 - 2.0 ** (-5.0 - h)) for h in heads]
    dmat = [jnp.where(rel >= 0, jnp.exp(jnp.maximum(rel, 0.0) * lg), 0.0) for lg in log_gamma]
    q_dec = [jnp.exp((pos + 1.0) * lg) for lg in log_gamma]
    k_dec = [jnp.exp((CHUNK - 1.0 - pos) * lg) * (HEAD_DIM ** -0.5) for lg in log_gamma]

    def rot(t, cos, sin):
        return t * cos + pltpu.roll(t, HEAD_DIM // 2, 1) * sin

    def silu(t):
        return t * _sigmoid(t)

    def ret_chunk(c):
        rows = slice(c * CHUNK, (c + 1) * CHUNK)
        cos = cos_ref[rows, :]
        sin = sin_ref[rows, :]
        q = [rot(qk_ref[rows, h * HEAD_DIM:(h + 1) * HEAD_DIM], cos, sin) for h in heads]
        k = [rot(qk_ref[rows, HW + h * HEAD_DIM:HW + (h + 1) * HEAD_DIM], cos, sin) for h in heads]
        v = [v_ref[rows, h * RET_DV:(h + 1) * RET_DV] for h in heads]
        a = [(_dot_nt(q[h].astype(BF16), (k[h] * (HEAD_DIM ** -0.5)).astype(BF16)) * dmat[h]).astype(BF16)
             for h in heads]
        state = [state_scr[h] for h in heads]
        o = [_dot(a[h], v[h]) + _dot((q[h] * q_dec[h]).astype(BF16), state[h].astype(BF16)) for h in heads]
        for h in heads:
            state_scr[h] = state[h] * math.exp(CHUNK * log_gamma[h]) + _dot_tn((k[h] * k_dec[h]).astype(BF16), v[h])
        for h in heads:
            mu = jnp.mean(o[h], axis=-1, keepdims=True)
            dev = o[h] - mu
            var = jnp.mean(dev * dev, axis=-1, keepdims=True)
            oc_scr[rows, h * RET_DV:(h + 1) * RET_DV] = dev * lax.rsqrt(var + NORM_EPS) * rnw

    hb = hb_ref[...]
    oa = (oa_ref[...] * silu(_dot(hb, wg_ref[:, 0:HW]))).astype(BF16)
    ob = ob_ref[...]
    gate = lambda j, lo: _sigmoid(_dot(hb, wg_ref[:, o_ma + j * d + lo:o_ma + j * d + lo + MERGE_CHUNK]))
    n_cols = d // MERGE_CHUNK
    n_ret = tm // CHUNK
    for c in range(max(n_cols, n_ret)):
        if c < n_cols:
            lo = c * MERGE_CHUNK
            cs = slice(lo, lo + MERGE_CHUNK)
            part_scr[:, cs] = (gate(0, lo) * _dot(oa, wbr_ref[0:HW, cs])
                               + gate(1, lo) * _dot(ob, wbr_ref[HW:2 * HW, cs]))
            gate_scr[:, cs] = gate(2, lo)
        if c < n_ret:
            ret_chunk(c)
    oc = (oc_scr[...] * silu(_dot(hb, wg_ref[:, HW:o_ma]))).astype(BF16)
    mix = None
    for c in range(n_cols):
        lo = c * MERGE_CHUNK
        cs = slice(lo, lo + MERGE_CHUNK)
        merged = part_scr[:, cs] + gate_scr[:, cs] * _dot(oc, wbr_ref[2 * HW:2 * HW + RV_W, cs])
        part = _dot(merged.astype(BF16), wo_ref[cs, :])
        mix = part if mix is None else mix + part
    y = _layer_norm(alpha * x_ref[...] + mix, g_ref[...], b_ref[...])
    o_ref[...] = y
    obf_ref[...] = y.astype(BF16)


def _merge_ret(x, hb, oa, ob, rqk, rv, cos, sin, layer, consts, alpha, s, tm=512):
    t, d = x.shape
    tm = min(tm, s)
    tiles_per_seq = s // tm
    acts = [x, hb, oa, ob, rqk, rv]
    table = pl.BlockSpec((tm, HEAD_DIM), lambda i: (i % tiles_per_seq, 0))
    return pl.pallas_call(
        functools.partial(_merge_ret_kernel, alpha, tiles_per_seq),
        grid=(t // tm,),
        in_specs=[pl.BlockSpec((tm, a.shape[1]), lambda i: (i, 0)) for a in acts] + [table, table]
        + [_layer_spec(w, layer) for w in consts],
        out_specs=[pl.BlockSpec((tm, d), lambda i: (i, 0)), pl.BlockSpec((tm, d), lambda i: (i, 0))],
        out_shape=[jax.ShapeDtypeStruct((t, d), F32), jax.ShapeDtypeStruct((t, d), BF16)],
        scratch_shapes=[pltpu.VMEM((HEADS, HEAD_DIM, RET_DV), F32), pltpu.VMEM((tm, RV_W), F32),
                        pltpu.VMEM((tm, d), F32), pltpu.VMEM((tm, d), F32)],
        compiler_params=_params("arbitrary"),
        name="retention_merge_ln",
    )(*acts, cos, sin, *consts)


REGROUP_ROWS = 512
SUBLANES = 8


def _regroup_kernel(scaled_block, wt_ref, o_ref):
    x = wt_ref[0]
    scale = jnp.where(pl.program_id(1) == scaled_block, HEAD_DIM ** -0.5, 1.0).astype(F32)
    o_ref[...] = (x * scale).T.astype(BF16)


def _regroup(w_in_t, runs, scaled_block=-1):
    depth, _, d = w_in_t.shape
    n_blocks = sum(n for _, n in runs)

    def src_row(j):
        tile, first = 0, 0
        for start, n in runs:
            tile = jnp.where(j >= first, start // SUBLANES + (j - first) * (REGROUP_ROWS // SUBLANES), tile)
            first += n
        return tile * SUBLANES

    return pl.pallas_call(
        functools.partial(_regroup_kernel, scaled_block),
        grid=(depth, n_blocks),
        in_specs=[pl.BlockSpec((pl.Element(1), pl.Element(REGROUP_ROWS), pl.Element(d)),
                               lambda l, j: (l, src_row(j), 0))],
        out_specs=pl.BlockSpec((None, d, REGROUP_ROWS), lambda l, j: (l, 0, j)),
        out_shape=jax.ShapeDtypeStruct((depth, d, n_blocks * REGROUP_ROWS), BF16),
        compiler_params=_params("parallel", "parallel"),
        name="regroup_w_in",
    )(w_in_t)


def _cast_rows_kernel(wt_ref, o_ref):
    o_ref[...] = wt_ref[0].astype(BF16)


def _cast_rows(w_in_t, start, n):
    depth, _, d = w_in_t.shape
    return pl.pallas_call(
        _cast_rows_kernel,
        grid=(depth,),
        in_specs=[pl.BlockSpec((pl.Element(1), pl.Element(n), pl.Element(d)), lambda l: (l, start, 0))],
        out_specs=pl.BlockSpec((None, n, d), lambda l: (l, 0, 0)),
        out_shape=jax.ShapeDtypeStruct((depth, n, d), BF16),
        compiler_params=_params("parallel"),
        name="cast_w_in_rows",
    )(w_in_t)


def _rotary_tables(s):
    half = HEAD_DIM // 2
    inv = ROPE_BASE ** (-jnp.arange(half, dtype=F32) / half)
    ang = jnp.arange(s).astype(F32)[:, None] * inv[None, :]
    cos, sin = jnp.cos(ang), jnp.sin(ang)
    return jnp.concatenate([cos, cos], axis=-1), jnp.concatenate([-sin, sin], axis=-1)


def kernel(x, p, ffn1_w13, ffn1_w2, w_in, gdn_conv_w, gdn_a_log, gdn_dt_bias, gdn_norm_w, ret_norm_w, w_br_gdn, w_br_sb, w_br_ret, w_out, ffn2_w13, ffn2_w2, ln_g, ln_b, w_ple_gate, w_ple_proj):
    bsz, s, d = x.shape
    depth = ffn1_w13.shape[0]
    t = bsz * s
    alpha = (2 * depth) ** 0.25
    bf = lambda a: a.astype(BF16)
    w_in_t = jnp.swapaxes(w_in, 1, 2)
    o_sq = 4 * HW + 2 * HEADS
    o_rg = o_sq + 5 * HW + RV_W
    nb = lambda width: width // REGROUP_ROWS
    w_proj = _regroup(w_in_t, [(0, nb(3 * HW)), (o_sq, nb(o_rg - o_sq))], scaled_block=nb(3 * HW))
    w_gate = _regroup(w_in_t, [(3 * HW, nb(HW)), (o_rg, nb(RV_W + 3 * d))])
    w_branch = bf(jnp.concatenate([w_br_gdn, w_br_sb, w_br_ret], axis=1))
    w_ba_t = _cast_rows(w_in_t, 4 * HW, 2 * HEADS)
    wb_t, wa_t = w_ba_t[:, :HEADS], w_ba_t[:, HEADS:]
    gdn_smalls = [jnp.swapaxes(wb_t, 1, 2), jnp.swapaxes(wa_t, 1, 2), wa_t,
                  gdn_a_log.reshape(depth, 1, HEADS), gdn_dt_bias.reshape(depth, 1, HEADS),
                  gdn_a_log.reshape(depth, HEADS, 1), gdn_dt_bias.reshape(depth, HEADS, 1),
                  gdn_norm_w.reshape(depth, 1, HEAD_DIM)]
    ret_nw = ret_norm_w.reshape(depth, 1, RET_DV)
    ffn1 = (bf(ffn1_w13), bf(ffn1_w2))
    ffn2 = (bf(ffn2_w13), bf(ffn2_w2))
    w_o, w_pg, w_pe = bf(w_out), bf(w_ple_gate), bf(w_ple_proj)
    ln = lambda a, j: a[:, j].reshape(depth, 1, d)
    p2 = p.reshape(depth, t, p.shape[-1])
    cos, sin = _rotary_tables(s)

    xf = x.reshape(t, d)
    for i in range(depth):
        xf, xb = _ffn(xf, i, *ffn1, ln(ln_g, 0), ln(ln_b, 0), alpha)
        xg, sqkv, rqk, rv = _proj(xb, i, w_proj, gdn_conv_w, s)
        oa = _gdn(xb.reshape(bsz, s, d), xg.reshape(bsz, s, 3 * HW), i, gdn_smalls)
        ob = _stick_breaking(sqkv.reshape(bsz, s, 3 * HW))
        xf, xb = _merge_ret(xf, xb, oa.reshape(t, HW), ob.reshape(t, HW), rqk, rv, cos, sin, i,
                            [ret_nw, w_gate, w_branch, w_o, ln(ln_g, 1), ln(ln_b, 1)], alpha, s)
        xf, xb = _ffn(xf, i, *ffn2, ln(ln_g, 2), ln(ln_b, 2), alpha, ple=(p2, w_pg, w_pe))
    return xf.reshape(bsz, s, d)
```

```python
import functools
import math

import jax
import jax.numpy as jnp
from jax import lax
from jax.experimental import pallas as pl
from jax.experimental.pallas import tpu as pltpu

F32 = jnp.float32
BF16 = jnp.bfloat16

NORM_EPS = 1e-5
HEADS = 4
HEAD_DIM = 128
RET_DV = 256
GDN_CONV = 4
ROPE_BASE = 10000.0
CHUNK = 128
CONV_HALO = 8
GDN_GROUP = 2
MERGE_CHUNK = 256
FFN_CHUNK = 256
PROJ_CHUNK = 256
SB_STICK_EXHAUSTED = 110.0
VMEM_LIMIT = 56 * 1024 * 1024

HW = HEADS * HEAD_DIM
RV_W = HEADS * RET_DV


def _dot(a, b):
    return jnp.dot(a, b, preferred_element_type=F32)


def _dot_nt(a, b):
    return lax.dot_general(a, b, (((1,), (1,)), ((), ())), preferred_element_type=F32)


def _dot_tn(a, b):
    return lax.dot_general(a, b, (((0,), (0,)), ((), ())), preferred_element_type=F32)


def _split(a):
    hi = a.astype(BF16)
    lo = (a - hi.astype(F32)).astype(BF16)
    return hi, lo


def _softplus(x):
    return jnp.maximum(x, 0.0) + jnp.log(1.0 + jnp.exp(-jnp.abs(x)))


def _sigmoid(x):
    return 1.0 / (1.0 + jnp.exp(-x))


def _layer_norm(y, g, b):
    mu = jnp.mean(y, axis=-1, keepdims=True)
    d = y - mu
    var = jnp.mean(d * d, axis=-1, keepdims=True)
    return d * lax.rsqrt(var + NORM_EPS) * g + b


def _params(*sem):
    return pltpu.CompilerParams(dimension_semantics=sem, vmem_limit_bytes=VMEM_LIMIT)


def _layer_spec(a, layer):
    zeros = (0,) * (a.ndim - 1)
    return pl.BlockSpec((None,) + a.shape[1:], lambda *_: (layer,) + zeros, pipeline_mode=pl.Buffered(1))


def _ffn_kernel(alpha, with_ple, x_ref, w13_ref, w2_ref, g_ref, b_ref, *rest):
    if with_ple:
        p_ref, wpg_ref, wpe_ref, o_ref, ob_ref = rest
    else:
        o_ref, ob_ref = rest
    f = w2_ref.shape[0]
    x = x_ref[...]
    xb = x.astype(BF16)
    acc = None
    for s in range(f // FFN_CHUNK):
        lo = s * FFN_CHUNK
        g = _dot(xb, w13_ref[:, lo:lo + FFN_CHUNK])
        u = _dot(xb, w13_ref[:, f + lo:f + lo + FFN_CHUNK])
        h = (g * _sigmoid(g) * u).astype(BF16)
        part = _dot(h, w2_ref[lo:lo + FFN_CHUNK, :])
        acc = part if acc is None else acc + part
    y = _layer_norm(alpha * x + 0.5 * acc, g_ref[...], b_ref[...])
    if with_ple:
        gate = _sigmoid(_dot(y.astype(BF16), wpg_ref[...]))
        y = y + gate * _dot(p_ref[...].astype(BF16), wpe_ref[...])
    o_ref[...] = y
    ob_ref[...] = y.astype(BF16)


def _ffn(x, layer, w13, w2, g, b, alpha, ple=None, tm=1024):
    t, d = x.shape
    tm = min(tm, t)
    consts = [w13, w2, g, b]
    in_specs = [pl.BlockSpec((tm, d), lambda i: (i, 0))] + [_layer_spec(a, layer) for a in consts]
    args = [x] + consts
    if ple is not None:
        p, wpg, wpe = ple
        in_specs += [pl.BlockSpec((None, tm, p.shape[2]), lambda i: (layer, i, 0)),
                     _layer_spec(wpg, layer), _layer_spec(wpe, layer)]
        args += [p, wpg, wpe]
    return pl.pallas_call(
        functools.partial(_ffn_kernel, alpha, ple is not None),
        grid=(t // tm,),
        in_specs=in_specs,
        out_specs=[pl.BlockSpec((tm, d), lambda i: (i, 0)),
                   pl.BlockSpec((tm, d), lambda i: (i, 0))],
        out_shape=[jax.ShapeDtypeStruct((t, d), F32), jax.ShapeDtypeStruct((t, d), BF16)],
        compiler_params=_params("parallel"),
        name="ffn_ln_ple" if ple is not None else "ffn_ln",
    )(*args)


def _proj_kernel(tiles_per_seq, hb_ref, w_ref, cw_ref, xg_ref, sb_ref, rqk_ref, rv_ref, xpad_scr):
    tm = hb_ref.shape[0]
    gw = xg_ref.shape[1]

    @pl.when(pl.program_id(0) % tiles_per_seq == 0)
    def _():
        xpad_scr[0:CONV_HALO, :] = jnp.zeros((CONV_HALO, xpad_scr.shape[1]), F32)

    hb = hb_ref[...]
    def conv_chunk(c):
        cs = slice(c * PROJ_CHUNK, (c + 1) * PROJ_CHUNK)
        xpad_scr[CONV_HALO:CONV_HALO + tm, cs] = _dot(hb, w_ref[:, cs])
        acc = cw_ref[GDN_CONV - 1:GDN_CONV, cs] * xpad_scr[CONV_HALO:CONV_HALO + tm, cs]
        for k in range(GDN_CONV - 1):
            off = CONV_HALO - (GDN_CONV - 1) + k
            acc = acc + cw_ref[k:k + 1, cs] * xpad_scr[off:off + tm, cs]
        xpad_scr[0:CONV_HALO, cs] = xpad_scr[tm:tm + CONV_HALO, cs]
        xg_ref[:, cs] = acc * _sigmoid(acc)

    plain = []
    lo = gw
    for ref in (sb_ref, rqk_ref, rv_ref):
        for c in range(ref.shape[1] // PROJ_CHUNK):
            plain.append((ref, c * PROJ_CHUNK, lo + c * PROJ_CHUNK))
        lo += ref.shape[1]
    n_conv = gw // PROJ_CHUNK
    per = -(-len(plain) // n_conv)
    for c in range(n_conv):
        conv_chunk(c)
        for ref, dst, src in plain[c * per:(c + 1) * per]:
            ref[:, dst:dst + PROJ_CHUNK] = _dot(hb, w_ref[:, src:src + PROJ_CHUNK]).astype(ref.dtype)


def _proj(hb, layer, w, conv_w, s, tm=512):
    t, d = hb.shape
    tm = min(tm, s)
    widths = [3 * HW, 3 * HW, 2 * HW, RV_W]
    dts = [F32, BF16, F32, BF16]
    return pl.pallas_call(
        functools.partial(_proj_kernel, s // tm),
        grid=(t // tm,),
        in_specs=[pl.BlockSpec((tm, d), lambda i: (i, 0)), _layer_spec(w, layer), _layer_spec(conv_w, layer)],
        out_specs=[pl.BlockSpec((tm, n), lambda i: (i, 0)) for n in widths],
        out_shape=[jax.ShapeDtypeStruct((t, n), dt) for n, dt in zip(widths, dts)],
        scratch_shapes=[pltpu.VMEM((tm + CONV_HALO, widths[0]), F32)],
        compiler_params=_params("arbitrary"),
        name="mixer_in_proj",
    )(hb, w, conv_w)


def _unit_lower_inverse(lows, ii, jj):
    eye = (ii == jj).astype(F32)
    diag8 = (ii >> 3) == (jj >> 3)
    xs = [jnp.where(diag8, -low, 0.0) for low in lows]
    xbs = [x.astype(BF16) for x in xs]
    x2s = [_dot(xb, xb).astype(BF16) for xb in xbs]
    ts = [eye + x for x in xs]
    ts = [t + _dot(t.astype(BF16), x2) for t, x2 in zip(ts, x2s)]
    x4s = [_dot(x2, x2).astype(BF16) for x2 in x2s]
    ts = [t + _dot(t.astype(BF16), x4) for t, x4 in zip(ts, x4s)]
    shift = 3
    while (1 << shift) < CHUNK:
        pair = ((ii >> (shift + 1)) == (jj >> (shift + 1))) & ((ii >> shift) != (jj >> shift))
        tbs = [t.astype(BF16) for t in ts]
        mids = [_dot(jnp.where(pair, low, 0.0).astype(BF16), tb).astype(BF16) for low, tb in zip(lows, tbs)]
        ts = [t - _dot(tb, mid) for t, tb, mid in zip(ts, tbs, mids)]
        shift += 1
    return ts


def _gdn_kernel(hb_ref, qkv_ref, wb_ref, wa_ref, wat_ref, alog_r_ref, dtb_r_ref,
                alog_c_ref, dtb_c_ref, nw_ref, o_ref, state_scr):
    bsz, tb, _ = qkv_ref.shape

    @pl.when(pl.program_id(0) == 0)
    def _():
        state_scr[...] = jnp.zeros_like(state_scr)

    ii = lax.broadcasted_iota(jnp.int32, (CHUNK, CHUNK), 0)
    jj = lax.broadcasted_iota(jnp.int32, (CHUNK, CHUNK), 1)
    tri_incl = (ii >= jj).astype(BF16)
    tri_incl_t = (ii <= jj).astype(BF16)
    neg_a_r = -jnp.exp(alog_r_ref[...])
    neg_a_c = -jnp.exp(alog_c_ref[...])
    nw = nw_ref[...]
    sq = (CHUNK, CHUNK)
    heads = range(HEADS)

    def l2n(t):
        return t * lax.rsqrt(jnp.sum(t * t, axis=-1, keepdims=True) + 1e-6)

    def local(rows):
        pairs = [(r, h) for r in rows for h in heads]
        cols, cum_rows, s_decs = [], [], []
        for b, r in rows:
            hb_c = hb_ref[b, r, :]
            beta_col = _sigmoid(_dot(hb_c, wb_ref[...]))
            g_col = neg_a_r * _softplus(_dot(hb_c, wa_ref[...]) + dtb_r_ref[...])
            g_row = neg_a_c * _softplus(_dot_nt(wat_ref[...], hb_c) + dtb_c_ref[...])
            gch, gcl = _split(g_col)
            grh, grl = _split(g_row)
            cum_col = _dot(tri_incl, gch) + _dot(tri_incl, gcl)
            cum_rows.append(_dot(grh, tri_incl_t) + _dot(grl, tri_incl_t))
            cum_last = cum_col[CHUNK - 1:CHUNK, :]
            cols.append((beta_col, cum_col, jnp.exp(cum_col), jnp.exp(cum_last - cum_col)))
            s_decs.append(jnp.exp(cum_last))

        n = len(pairs)
        col = lambda j, which, h: jnp.broadcast_to(cols[j][which][:, h:h + 1], sq)
        chunk_of = [j for j in range(len(rows)) for _ in heads]
        head_of = [h for _ in rows for h in heads]
        q = [l2n(qkv_ref[b, r, h * HEAD_DIM:(h + 1) * HEAD_DIM]) * (HEAD_DIM ** -0.5) for (b, r), h in pairs]
        k = [l2n(qkv_ref[b, r, HW + h * HEAD_DIM:HW + (h + 1) * HEAD_DIM]) for (b, r), h in pairs]
        v = [qkv_ref[b, r, 2 * HW + h * HEAD_DIM:2 * HW + (h + 1) * HEAD_DIM] for (b, r), h in pairs]
        beta = [col(chunk_of[p], 0, head_of[p]) for p in range(n)]
        e_g = [col(chunk_of[p], 2, head_of[p]) for p in range(n)]
        dec = [jnp.exp(jnp.where(ii >= jj, col(chunk_of[p], 1, head_of[p])
                                 - jnp.broadcast_to(cum_rows[chunk_of[p]][head_of[p]:head_of[p] + 1, :], sq), -1e30))
               for p in range(n)]
        kb = [k[p] * beta[p] for p in range(n)]
        kbf = [k[p].astype(BF16) for p in range(n)]
        kq = [_dot_nt(jnp.concatenate([kb[p], q[p]], axis=0).astype(BF16), kbf[p]) for p in range(n)]
        low = [jnp.where(ii > jj, kq[p][:CHUNK] * dec[p], 0.0) for p in range(n)]
        t_inv = _unit_lower_inverse(low, ii, jj)
        rhs = [jnp.concatenate([v[p] * beta[p], kb[p] * e_g[p]], axis=1).astype(BF16) for p in range(n)]
        sol = [_dot(t_inv[p].astype(BF16), rhs[p]) for p in range(n)]
        a_qk = [(kq[p][CHUNK:] * dec[p]).astype(BF16) for p in range(n)]
        wq = [jnp.concatenate([sol[p][:, HEAD_DIM:], q[p] * e_g[p]], axis=0).astype(BF16) for p in range(n)]
        u = [sol[p][:, :HEAD_DIM] for p in range(n)]
        k_dec = [(k[p] * col(chunk_of[p], 3, head_of[p])).astype(BF16) for p in range(n)]
        return u, wq, a_qk, k_dec, s_decs

    def recurrent(rows, c0, u, wq, a_qk, k_dec, s_decs):
        chains = [(i, h) for i in range(len(rows)) for h in heads]
        state = [state_scr[rows[i][0] * HEADS + h] for i, h in chains]
        pair = [(c0 + i) * HEADS + h for i, h in chains]
        from_state = [_dot(wq[p], s.astype(BF16)) for p, s in zip(pair, state)]
        v_new = [(u[p] - fs[:CHUNK]).astype(BF16) for p, fs in zip(pair, from_state)]
        o = [fs[CHUNK:] + _dot(a_qk[p], vn) for p, fs, vn in zip(pair, from_state, v_new)]
        for (i, h), p, s, vn in zip(chains, pair, state, v_new):
            state_scr[rows[i][0] * HEADS + h] = s * s_decs[c0 + i][:, h:h + 1] + _dot_tn(k_dec[p], vn)
        for (i, h), on in zip(chains, o):
            on = on * lax.rsqrt(jnp.mean(on * on, axis=-1, keepdims=True) + NORM_EPS) * nw
            o_ref[rows[i][0], rows[i][1], h * HEAD_DIM:(h + 1) * HEAD_DIM] = on

    def chunk_group(c, carry):
        slices = [pl.ds(pl.multiple_of((c * GDN_GROUP + j) * CHUNK, CHUNK), CHUNK) for j in range(GDN_GROUP)]
        rows = [(b, r) for r in slices for b in range(bsz)]
        parts = local(rows)
        for j in range(GDN_GROUP):
            recurrent(rows[j * bsz:(j + 1) * bsz], j * bsz, *parts)
        return carry

    lax.fori_loop(0, tb // (CHUNK * GDN_GROUP), chunk_group, 0)


def _gdn(hb, xg, layer, smalls, tb=1024):
    bsz, s, d = hb.shape
    tb = min(tb, s)
    return pl.pallas_call(
        _gdn_kernel,
        grid=(s // tb,),
        in_specs=[pl.BlockSpec((bsz, tb, d), lambda i: (0, i, 0)),
                  pl.BlockSpec((bsz, tb, 3 * HW), lambda i: (0, i, 0))] + [_layer_spec(a, layer) for a in smalls],
        out_specs=pl.BlockSpec((bsz, tb, HW), lambda i: (0, i, 0)),
        out_shape=jax.ShapeDtypeStruct((bsz, s, HW), F32),
        scratch_shapes=[pltpu.VMEM((bsz * HEADS, HEAD_DIM, HEAD_DIM), F32)],
        compiler_params=_params("arbitrary"),
        name="gated_deltanet",
    )(hb, xg, *smalls)


def _sb_kernel(q_ref, k_ref, v_ref, o_ref, out_scr, spent_scr):
    tq = q_ref.shape[1]
    tk = tq
    qi = pl.program_id(1)
    heads = range(HEADS)
    hs = lambda h: slice(h * HEAD_DIM, (h + 1) * HEAD_DIM)
    row = lax.broadcasted_iota(jnp.int32, (tq, tk), 0)
    col = lax.broadcasted_iota(jnp.int32, (tq, tk), 1)
    suffix = (row >= col).astype(BF16)
    suffix2 = jnp.concatenate([suffix, suffix], axis=0)
    q = [q_ref[0, :, hs(h)] for h in heads]
    wide = lambda a: jnp.concatenate([a] * (tk // HEAD_DIM), axis=1)

    def block(kj, spent, valid):
        keys = pl.ds(pl.multiple_of(kj * tk, tk), tk)
        z = [_dot_nt(q[h], k_ref[0, keys, hs(h)]) for h in heads]
        sp = [_softplus(z[h]) for h in heads]
        nlk = sp if valid is None else [jnp.where(valid, sp[h], 0.0) for h in heads]
        sums = [_dot(jnp.concatenate(_split(nlk[h]), axis=1), suffix2) for h in heads]
        wgt = [jnp.exp(z[h] - sums[h] - wide(spent[h])) for h in heads]
        if valid is not None:
            wgt = [jnp.where(valid, wgt[h], 0.0) for h in heads]
        pv = [_dot(wgt[h].astype(BF16), v_ref[0, keys, hs(h)]) for h in heads]
        spent = [spent[h] + jnp.broadcast_to(sums[h][:, 0:1], (tq, HEAD_DIM)) for h in heads]
        least = jnp.minimum(jnp.minimum(spent[0], spent[1]), jnp.minimum(spent[2], spent[3]))
        return pv, spent, jnp.min(least)

    zero = jnp.zeros((tq, HEAD_DIM), F32)
    pv, spent, least = block(qi, [zero] * HEADS, col < row)
    for h in heads:
        out_scr[:, hs(h)] = pv[h]
        spent_scr[h] = spent[h]

    def cond(carry):
        kj, spent_min = carry
        return jnp.logical_and(kj >= 0, spent_min < SB_STICK_EXHAUSTED)

    def body(carry):
        kj, _ = carry
        pv, spent, least = block(kj, [spent_scr[h] for h in heads], None)
        for h in heads:
            out_scr[:, hs(h)] += pv[h]
            spent_scr[h] = spent[h]
        return kj - 1, least

    lax.while_loop(cond, body, (qi - 1, least))
    o_ref[0] = out_scr[...].astype(o_ref.dtype)


def _stick_breaking(qkv, tq=256):
    bsz, s, _ = qkv.shape
    tq = min(tq, s)
    resident = lambda j: pl.BlockSpec((1, s, HW), lambda b, i: (b, 0, j), pipeline_mode=pl.Buffered(1))
    return pl.pallas_call(
        _sb_kernel,
        grid=(bsz, s // tq),
        in_specs=[pl.BlockSpec((1, tq, HW), lambda b, i: (b, i, 0)), resident(1), resident(2)],
        out_specs=pl.BlockSpec((1, tq, HW), lambda b, i: (b, i, 0)),
        out_shape=jax.ShapeDtypeStruct((bsz, s, HW), BF16),
        scratch_shapes=[pltpu.VMEM((tq, HW), F32), pltpu.VMEM((HEADS, tq, HEAD_DIM), F32)],
        compiler_params=_params("parallel", "arbitrary"),
        name="stick_breaking",
    )(qkv, qkv, qkv)


def _merge_ret_kernel(alpha, tiles_per_seq, x_ref, hb_ref, oa_ref, ob_ref, qk_ref, v_ref, cos_ref, sin_ref,
                      rnw_ref, wg_ref, wbr_ref, wo_ref, g_ref, b_ref, o_ref, obf_ref,
                      state_scr, oc_scr, part_scr, gate_scr):
    tm = x_ref.shape[0]
    d = wo_ref.shape[0]
    heads = range(HEADS)
    o_ma = HW + RV_W

    @pl.when(pl.program_id(0) % tiles_per_seq == 0)
    def _():
        state_scr[...] = jnp.zeros_like(state_scr)

    ii = lax.broadcasted_iota(jnp.int32, (CHUNK, CHUNK), 0)
    jj = lax.broadcasted_iota(jnp.int32, (CHUNK, CHUNK), 1)
    rel = (ii - jj).astype(F32)
    pos = ii.astype(F32)
    rnw = rnw_ref[...]
    log_gamma = [math.log(1.0 - 2.0 ** (-5.0 - h)) for h in heads]
    dmat = [jnp.where(rel >= 0, jnp.exp(jnp.maximum(rel, 0.0) * lg), 0.0) for lg in log_gamma]
    q_dec = [jnp.exp((pos + 1.0) * lg) for lg in log_gamma]
    k_dec = [jnp.exp((CHUNK - 1.0 - pos) * lg) * (HEAD_DIM ** -0.5) for lg in log_gamma]

    def rot(t, cos, sin):
        return t * cos + pltpu.roll(t, HEAD_DIM // 2, 1) * sin

    def silu(t):
        return t * _sigmoid(t)

    def ret_chunk(c):
        rows = slice(c * CHUNK, (c + 1) * CHUNK)
        cos = cos_ref[rows, :]
        sin = sin_ref[rows, :]
        q = [rot(qk_ref[rows, h * HEAD_DIM:(h + 1) * HEAD_DIM], cos, sin) for h in heads]
        k = [rot(qk_ref[rows, HW + h * HEAD_DIM:HW + (h + 1) * HEAD_DIM], cos, sin) for h in heads]
        v = [v_ref[rows, h * RET_DV:(h + 1) * RET_DV] for h in heads]
        a = [(_dot_nt(q[h].astype(BF16), (k[h] * (HEAD_DIM ** -0.5)).astype(BF16)) * dmat[h]).astype(BF16)
             for h in heads]
        state = [state_scr[h] for h in heads]
        o = [_dot(a[h], v[h]) + _dot((q[h] * q_dec[h]).astype(BF16), state[h].astype(BF16)) for h in heads]
        for h in heads:
            state_scr[h] = state[h] * math.exp(CHUNK * log_gamma[h]) + _dot_tn((k[h] * k_dec[h]).astype(BF16), v[h])
        for h in heads:
            mu = jnp.mean(o[h], axis=-1, keepdims=True)
            dev = o[h] - mu
            var = jnp.mean(dev * dev, axis=-1, keepdims=True)
            oc_scr[rows, h * RET_DV:(h + 1) * RET_DV] = dev * lax.rsqrt(var + NORM_EPS) * rnw

    hb = hb_ref[...]
    oa = (oa_ref[...] * silu(_dot(hb, wg_ref[:, 0:HW]))).astype(BF16)
    ob = ob_ref[...]
    gate = lambda j, lo: _sigmoid(_dot(hb, wg_ref[:, o_ma + j * d + lo:o_ma + j * d + lo + MERGE_CHUNK]))
    n_cols = d // MERGE_CHUNK
    n_ret = tm // CHUNK
    for c in range(max(n_cols, n_ret)):
        if c < n_cols:
            lo = c * MERGE_CHUNK
            cs = slice(lo, lo + MERGE_CHUNK)
            part_scr[:, cs] = (gate(0, lo) * _dot(oa, wbr_ref[0:HW, cs])
                               + gate(1, lo) * _dot(ob, wbr_ref[HW:2 * HW, cs]))
            gate_scr[:, cs] = gate(2, lo)
        if c < n_ret:
            ret_chunk(c)
    oc = (oc_scr[...] * silu(_dot(hb, wg_ref[:, HW:o_ma]))).astype(BF16)
    mix = None
    for c in range(n_cols):
        lo = c * MERGE_CHUNK
        cs = slice(lo, lo + MERGE_CHUNK)
        merged = part_scr[:, cs] + gate_scr[:, cs] * _dot(oc, wbr_ref[2 * HW:2 * HW + RV_W, cs])
        part = _dot(merged.astype(BF16), wo_ref[cs, :])
        mix = part if mix is None else mix + part
    y = _layer_norm(alpha * x_ref[...] + mix, g_ref[...], b_ref[...])
    o_ref[...] = y
    obf_ref[...] = y.astype(BF16)


def _merge_ret(x, hb, oa, ob, rqk, rv, cos, sin, layer, consts, alpha, s, tm=512):
    t, d = x.shape
    tm = min(tm, s)
    tiles_per_seq = s // tm
    acts = [x, hb, oa, ob, rqk, rv]
    table = pl.BlockSpec((tm, HEAD_DIM), lambda i: (i % tiles_per_seq, 0))
    return pl.pallas_call(
        functools.partial(_merge_ret_kernel, alpha, tiles_per_seq),
        grid=(t // tm,),
        in_specs=[pl.BlockSpec((tm, a.shape[1]), lambda i: (i, 0)) for a in acts] + [table, table]
        + [_layer_spec(w, layer) for w in consts],
        out_specs=[pl.BlockSpec((tm, d), lambda i: (i, 0)), pl.BlockSpec((tm, d), lambda i: (i, 0))],
        out_shape=[jax.ShapeDtypeStruct((t, d), F32), jax.ShapeDtypeStruct((t, d), BF16)],
        scratch_shapes=[pltpu.VMEM((HEADS, HEAD_DIM, RET_DV), F32), pltpu.VMEM((tm, RV_W), F32),
                        pltpu.VMEM((tm, d), F32), pltpu.VMEM((tm, d), F32)],
        compiler_params=_params("arbitrary"),
        name="retention_merge_ln",
    )(*acts, cos, sin, *consts)


REGROUP_ROWS = 512
SUBLANES = 8


def _regroup_kernel(scaled_block, wt_ref, o_ref):
    x = wt_ref[0]
    scale = jnp.where(pl.program_id(1) == scaled_block, HEAD_DIM ** -0.5, 1.0).astype(F32)
    o_ref[...] = (x * scale).T.astype(BF16)


def _regroup(w_in_t, runs, scaled_block=-1):
    depth, _, d = w_in_t.shape
    n_blocks = sum(n for _, n in runs)

    def src_row(j):
        tile, first = 0, 0
        for start, n in runs:
            tile = jnp.where(j >= first, start // SUBLANES + (j - first) * (REGROUP_ROWS // SUBLANES), tile)
            first += n
        return tile * SUBLANES

    return pl.pallas_call(
        functools.partial(_regroup_kernel, scaled_block),
        grid=(depth, n_blocks),
        in_specs=[pl.BlockSpec((pl.Element(1), pl.Element(REGROUP_ROWS), pl.Element(d)),
                               lambda l, j: (l, src_row(j), 0))],
        out_specs=pl.BlockSpec((None, d, REGROUP_ROWS), lambda l, j: (l, 0, j)),
        out_shape=jax.ShapeDtypeStruct((depth, d, n_blocks * REGROUP_ROWS), BF16),
        compiler_params=_params("parallel", "parallel"),
        name="regroup_w_in",
    )(w_in_t)


def _cast_rows_kernel(wt_ref, o_ref):
    o_ref[...] = wt_ref[0].astype(BF16)


def _cast_rows(w_in_t, start, n):
    depth, _, d = w_in_t.shape
    return pl.pallas_call(
        _cast_rows_kernel,
        grid=(depth,),
        in_specs=[pl.BlockSpec((pl.Element(1), pl.Element(n), pl.Element(d)), lambda l: (l, start, 0))],
        out_specs=pl.BlockSpec((None, n, d), lambda l: (l, 0, 0)),
        out_shape=jax.ShapeDtypeStruct((depth, n, d), BF16),
        compiler_params=_params("parallel"),
        name="cast_w_in_rows",
    )(w_in_t)


def _rotary_tables(s):
    half = HEAD_DIM // 2
    inv = ROPE_BASE ** (-jnp.arange(half, dtype=F32) / half)
    ang = jnp.arange(s).astype(F32)[:, None] * inv[None, :]
    cos, sin = jnp.cos(ang), jnp.sin(ang)
    return jnp.concatenate([cos, cos], axis=-1), jnp.concatenate([-sin, sin], axis=-1)


def kernel(x, p, ffn1_w13, ffn1_w2, w_in, gdn_conv_w, gdn_a_log, gdn_dt_bias, gdn_norm_w, ret_norm_w, w_br_gdn, w_br_sb, w_br_ret, w_out, ffn2_w13, ffn2_w2, ln_g, ln_b, w_ple_gate, w_ple_proj):
    bsz, s, d = x.shape
    depth = ffn1_w13.shape[0]
    t = bsz * s
    alpha = (2 * depth) ** 0.25
    bf = lambda a: a.astype(BF16)
    w_in_t = jnp.swapaxes(w_in, 1, 2)
    o_sq = 4 * HW + 2 * HEADS
    o_rg = o_sq + 5 * HW + RV_W
    nb = lambda width: width // REGROUP_ROWS
    w_proj = _regroup(w_in_t, [(0, nb(3 * HW)), (o_sq, nb(o_rg - o_sq))], scaled_block=nb(3 * HW))
    w_gate = _regroup(w_in_t, [(3 * HW, nb(HW)), (o_rg, nb(RV_W + 3 * d))])
    w_branch = bf(jnp.concatenate([w_br_gdn, w_br_sb, w_br_ret], axis=1))
    w_ba_t = _cast_rows(w_in_t, 4 * HW, 2 * HEADS)
    wb_t, wa_t = w_ba_t[:, :HEADS], w_ba_t[:, HEADS:]
    gdn_smalls = [jnp.swapaxes(wb_t, 1, 2), jnp.swapaxes(wa_t, 1, 2), wa_t,
                  gdn_a_log.reshape(depth, 1, HEADS), gdn_dt_bias.reshape(depth, 1, HEADS),
                  gdn_a_log.reshape(depth, HEADS, 1), gdn_dt_bias.reshape(depth, HEADS, 1),
                  gdn_norm_w.reshape(depth, 1, HEAD_DIM)]
    ret_nw = ret_norm_w.reshape(depth, 1, RET_DV)
    ffn1 = (bf(ffn1_w13), bf(ffn1_w2))
    ffn2 = (bf(ffn2_w13), bf(ffn2_w2))
    w_o, w_pg, w_pe = bf(w_out), bf(w_ple_gate), bf(w_ple_proj)
    ln = lambda a, j: a[:, j].reshape(depth, 1, d)
    p2 = p.reshape(depth, t, p.shape[-1])
    cos, sin = _rotary_tables(s)

    xf = x.reshape(t, d)
    for i in range(depth):
        xf, xb = _ffn(xf, i, *ffn1, ln(ln_g, 0), ln(ln_b, 0), alpha)
        xg, sqkv, rqk, rv = _proj(xb, i, w_proj, gdn_conv_w, s)
        oa = _gdn(xb.reshape(bsz, s, d), xg.reshape(bsz, s, 3 * HW), i, gdn_smalls)
        ob = _stick_breaking(sqkv.reshape(bsz, s, 3 * HW))
        xf, xb = _merge_ret(xf, xb, oa.reshape(t, HW), ob.reshape(t, HW), rqk, rv, cos, sin, i,
                            [ret_nw, w_gate, w_branch, w_o, ln(ln_g, 1), ln(ln_b, 1)], alpha, s)
        xf, xb = _ffn(xf, i, *ffn2, ln(ln_g, 2), ln(ln_b, 2), alpha, ple=(p2, w_pg, w_pe))
    return xf.reshape(bsz, s, d)
```

```python
import functools
import math

import jax
import jax.numpy as jnp
from jax import lax
from jax.experimental import pallas as pl
from jax.experimental.pallas import tpu as pltpu

F32 = jnp.float32
BF16 = jnp.bfloat16

NORM_EPS = 1e-5
HEADS = 4
HEAD_DIM = 128
RET_DV = 256
GDN_CONV = 4
ROPE_BASE = 10000.0
CHUNK = 128
CONV_HALO = 8
GDN_GROUP = 2
MERGE_CHUNK = 512
FFN_CHUNK = 256
PROJ_CHUNK = 256
SB_STICK_EXHAUSTED = 110.0
VMEM_LIMIT = 56 * 1024 * 1024

HW = HEADS * HEAD_DIM
RV_W = HEADS * RET_DV


def _dot(a, b):
    return jnp.dot(a, b, preferred_element_type=F32)


def _dot_nt(a, b):
    return lax.dot_general(a, b, (((1,), (1,)), ((), ())), preferred_element_type=F32)


def _dot_tn(a, b):
    return lax.dot_general(a, b, (((0,), (0,)), ((), ())), preferred_element_type=F32)


def _split(a):
    hi = a.astype(BF16)
    lo = (a - hi.astype(F32)).astype(BF16)
    return hi, lo


def _softplus(x):
    return jnp.maximum(x, 0.0) + jnp.log(1.0 + jnp.exp(-jnp.abs(x)))


def _sigmoid(x):
    return 1.0 / (1.0 + jnp.exp(-x))


def _layer_norm(y, g, b):
    mu = jnp.mean(y, axis=-1, keepdims=True)
    d = y - mu
    var = jnp.mean(d * d, axis=-1, keepdims=True)
    return d * lax.rsqrt(var + NORM_EPS) * g + b


def _params(*sem):
    return pltpu.CompilerParams(dimension_semantics=sem, vmem_limit_bytes=VMEM_LIMIT)


def _layer_spec(a, layer):
    zeros = (0,) * (a.ndim - 1)
    return pl.BlockSpec((None,) + a.shape[1:], lambda *_: (layer,) + zeros, pipeline_mode=pl.Buffered(1))


def _ffn_kernel(alpha, with_ple, x_ref, w13_ref, w2_ref, g_ref, b_ref, *rest):
    if with_ple:
        p_ref, wpg_ref, wpe_ref, o_ref, ob_ref = rest
    else:
        o_ref, ob_ref = rest
    f = w2_ref.shape[0]
    x = x_ref[...]
    xb = x.astype(BF16)
    acc = None
    for s in range(f // FFN_CHUNK):
        lo = s * FFN_CHUNK
        g = _dot(xb, w13_ref[:, lo:lo + FFN_CHUNK])
        u = _dot(xb, w13_ref[:, f + lo:f + lo + FFN_CHUNK])
        h = (g * _sigmoid(g) * u).astype(BF16)
        part = _dot(h, w2_ref[lo:lo + FFN_CHUNK, :])
        acc = part if acc is None else acc + part
    y = _layer_norm(alpha * x + 0.5 * acc, g_ref[...], b_ref[...])
    if with_ple:
        gate = _sigmoid(_dot(y.astype(BF16), wpg_ref[...]))
        y = y + gate * _dot(p_ref[...].astype(BF16), wpe_ref[...])
    o_ref[...] = y
    ob_ref[...] = y.astype(BF16)


def _ffn(x, layer, w13, w2, g, b, alpha, ple=None, tm=1024):
    t, d = x.shape
    tm = min(tm, t)
    consts = [w13, w2, g, b]
    in_specs = [pl.BlockSpec((tm, d), lambda i: (i, 0))] + [_layer_spec(a, layer) for a in consts]
    args = [x] + consts
    if ple is not None:
        p, wpg, wpe = ple
        in_specs += [pl.BlockSpec((None, tm, p.shape[2]), lambda i: (layer, i, 0)),
                     _layer_spec(wpg, layer), _layer_spec(wpe, layer)]
        args += [p, wpg, wpe]
    return pl.pallas_call(
        functools.partial(_ffn_kernel, alpha, ple is not None),
        grid=(t // tm,),
        in_specs=in_specs,
        out_specs=[pl.BlockSpec((tm, d), lambda i: (i, 0)),
                   pl.BlockSpec((tm, d), lambda i: (i, 0))],
        out_shape=[jax.ShapeDtypeStruct((t, d), F32), jax.ShapeDtypeStruct((t, d), BF16)],
        compiler_params=_params("parallel"),
        name="ffn_ln_ple" if ple is not None else "ffn_ln",
    )(*args)


def _proj_kernel(tiles_per_seq, hb_ref, w_ref, cw_ref, xg_ref, sb_ref, rqk_ref, rv_ref, xpad_scr):
    tm = hb_ref.shape[0]
    gw = xg_ref.shape[1]

    @pl.when(pl.program_id(0) % tiles_per_seq == 0)
    def _():
        xpad_scr[0:CONV_HALO, :] = jnp.zeros((CONV_HALO, xpad_scr.shape[1]), F32)

    hb = hb_ref[...]
    def conv_chunk(c):
        cs = slice(c * PROJ_CHUNK, (c + 1) * PROJ_CHUNK)
        xpad_scr[CONV_HALO:CONV_HALO + tm, cs] = _dot(hb, w_ref[:, cs])
        acc = cw_ref[GDN_CONV - 1:GDN_CONV, cs] * xpad_scr[CONV_HALO:CONV_HALO + tm, cs]
        for k in range(GDN_CONV - 1):
            off = CONV_HALO - (GDN_CONV - 1) + k
            acc = acc + cw_ref[k:k + 1, cs] * xpad_scr[off:off + tm, cs]
        xpad_scr[0:CONV_HALO, cs] = xpad_scr[tm:tm + CONV_HALO, cs]
        xg_ref[:, cs] = acc * _sigmoid(acc)

    plain = []
    lo = gw
    for ref in (sb_ref, rqk_ref, rv_ref):
        for c in range(ref.shape[1] // PROJ_CHUNK):
            plain.append((ref, c * PROJ_CHUNK, lo + c * PROJ_CHUNK))
        lo += ref.shape[1]
    n_conv = gw // PROJ_CHUNK
    per = -(-len(plain) // n_conv)
    for c in range(n_conv):
        conv_chunk(c)
        for ref, dst, src in plain[c * per:(c + 1) * per]:
            ref[:, dst:dst + PROJ_CHUNK] = _dot(hb, w_ref[:, src:src + PROJ_CHUNK]).astype(ref.dtype)


def _proj(hb, layer, w, conv_w, s, tm=512):
    t, d = hb.shape
    tm = min(tm, s)
    widths = [3 * HW, 3 * HW, 2 * HW, RV_W]
    dts = [F32, BF16, F32, BF16]
    return pl.pallas_call(
        functools.partial(_proj_kernel, s // tm),
        grid=(t // tm,),
        in_specs=[pl.BlockSpec((tm, d), lambda i: (i, 0)), _layer_spec(w, layer), _layer_spec(conv_w, layer)],
        out_specs=[pl.BlockSpec((tm, n), lambda i: (i, 0)) for n in widths],
        out_shape=[jax.ShapeDtypeStruct((t, n), dt) for n, dt in zip(widths, dts)],
        scratch_shapes=[pltpu.VMEM((tm + CONV_HALO, widths[0]), F32)],
        compiler_params=_params("arbitrary"),
        name="mixer_in_proj",
    )(hb, w, conv_w)


def _unit_lower_inverse(lows, ii, jj):
    eye = (ii == jj).astype(F32)
    diag8 = (ii >> 3) == (jj >> 3)
    xs = [jnp.where(diag8, -low, 0.0) for low in lows]
    xbs = [x.astype(BF16) for x in xs]
    x2s = [_dot(xb, xb).astype(BF16) for xb in xbs]
    ts = [eye + x for x in xs]
    ts = [t + _dot(t.astype(BF16), x2) for t, x2 in zip(ts, x2s)]
    x4s = [_dot(x2, x2).astype(BF16) for x2 in x2s]
    ts = [t + _dot(t.astype(BF16), x4) for t, x4 in zip(ts, x4s)]
    shift = 3
    while (1 << shift) < CHUNK:
        pair = ((ii >> (shift + 1)) == (jj >> (shift + 1))) & ((ii >> shift) != (jj >> shift))
        tbs = [t.astype(BF16) for t in ts]
        mids = [_dot(jnp.where(pair, low, 0.0).astype(BF16), tb).astype(BF16) for low, tb in zip(lows, tbs)]
        ts = [t - _dot(tb, mid) for t, tb, mid in zip(ts, tbs, mids)]
        shift += 1
    return ts


def _gdn_kernel(hb_ref, qkv_ref, wb_ref, wa_ref, wat_ref, alog_r_ref, dtb_r_ref,
                alog_c_ref, dtb_c_ref, nw_ref, o_ref, state_scr):
    bsz, tb, _ = qkv_ref.shape

    @pl.when(pl.program_id(0) == 0)
    def _():
        state_scr[...] = jnp.zeros_like(state_scr)

    ii = lax.broadcasted_iota(jnp.int32, (CHUNK, CHUNK), 0)
    jj = lax.broadcasted_iota(jnp.int32, (CHUNK, CHUNK), 1)
    tri_incl = (ii >= jj).astype(BF16)
    tri_incl_t = (ii <= jj).astype(BF16)
    neg_a_r = -jnp.exp(alog_r_ref[...])
    neg_a_c = -jnp.exp(alog_c_ref[...])
    nw = nw_ref[...]
    sq = (CHUNK, CHUNK)
    heads = range(HEADS)

    def l2n(t):
        return t * lax.rsqrt(jnp.sum(t * t, axis=-1, keepdims=True) + 1e-6)

    def local(rows):
        pairs = [(r, h) for r in rows for h in heads]
        cols, cum_rows, s_decs = [], [], []
        for b, r in rows:
            hb_c = hb_ref[b, r, :]
            beta_col = _sigmoid(_dot(hb_c, wb_ref[...]))
            g_col = neg_a_r * _softplus(_dot(hb_c, wa_ref[...]) + dtb_r_ref[...])
            g_row = neg_a_c * _softplus(_dot_nt(wat_ref[...], hb_c) + dtb_c_ref[...])
            gch, gcl = _split(g_col)
            grh, grl = _split(g_row)
            cum_col = _dot(tri_incl, gch) + _dot(tri_incl, gcl)
            cum_rows.append(_dot(grh, tri_incl_t) + _dot(grl, tri_incl_t))
            cum_last = cum_col[CHUNK - 1:CHUNK, :]
            cols.append((beta_col, cum_col, jnp.exp(cum_col), jnp.exp(cum_last - cum_col)))
            s_decs.append(jnp.exp(cum_last))

        n = len(pairs)
        col = lambda j, which, h: jnp.broadcast_to(cols[j][which][:, h:h + 1], sq)
        chunk_of = [j for j in range(len(rows)) for _ in heads]
        head_of = [h for _ in rows for h in heads]
        q = [l2n(qkv_ref[b, r, h * HEAD_DIM:(h + 1) * HEAD_DIM]) * (HEAD_DIM ** -0.5) for (b, r), h in pairs]
        k = [l2n(qkv_ref[b, r, HW + h * HEAD_DIM:HW + (h + 1) * HEAD_DIM]) for (b, r), h in pairs]
        v = [qkv_ref[b, r, 2 * HW + h * HEAD_DIM:2 * HW + (h + 1) * HEAD_DIM] for (b, r), h in pairs]
        beta = [col(chunk_of[p], 0, head_of[p]) for p in range(n)]
        e_g = [col(chunk_of[p], 2, head_of[p]) for p in range(n)]
        dec = [jnp.exp(jnp.where(ii >= jj, col(chunk_of[p], 1, head_of[p])
                                 - jnp.broadcast_to(cum_rows[chunk_of[p]][head_of[p]:head_of[p] + 1, :], sq), -1e30))
               for p in range(n)]
        kb = [k[p] * beta[p] for p in range(n)]
        kbf = [k[p].astype(BF16) for p in range(n)]
        kq = [_dot_nt(jnp.concatenate([kb[p], q[p]], axis=0).astype(BF16), kbf[p]) for p in range(n)]
        low = [jnp.where(ii > jj, kq[p][:CHUNK] * dec[p], 0.0) for p in range(n)]
        t_inv = _unit_lower_inverse(low, ii, jj)
        rhs = [jnp.concatenate([v[p] * beta[p], kb[p] * e_g[p]], axis=1).astype(BF16) for p in range(n)]
        sol = [_dot(t_inv[p].astype(BF16), rhs[p]) for p in range(n)]
        a_qk = [(kq[p][CHUNK:] * dec[p]).astype(BF16) for p in range(n)]
        wq = [jnp.concatenate([sol[p][:, HEAD_DIM:], q[p] * e_g[p]], axis=0).astype(BF16) for p in range(n)]
        u = [sol[p][:, :HEAD_DIM] for p in range(n)]
        k_dec = [(k[p] * col(chunk_of[p], 3, head_of[p])).astype(BF16) for p in range(n)]
        return u, wq, a_qk, k_dec, s_decs

    def recurrent(rows, c0, u, wq, a_qk, k_dec, s_decs):
        chains = [(i, h) for i in range(len(rows)) for h in heads]
        state = [state_scr[rows[i][0] * HEADS + h] for i, h in chains]
        pair = [(c0 + i) * HEADS + h for i, h in chains]
        from_state = [_dot(wq[p], s.astype(BF16)) for p, s in zip(pair, state)]
        v_new = [(u[p] - fs[:CHUNK]).astype(BF16) for p, fs in zip(pair, from_state)]
        o = [fs[CHUNK:] + _dot(a_qk[p], vn) for p, fs, vn in zip(pair, from_state, v_new)]
        for (i, h), p, s, vn in zip(chains, pair, state, v_new):
            state_scr[rows[i][0] * HEADS + h] = s * s_decs[c0 + i][:, h:h + 1] + _dot_tn(k_dec[p], vn)
        for (i, h), on in zip(chains, o):
            on = on * lax.rsqrt(jnp.mean(on * on, axis=-1, keepdims=True) + NORM_EPS) * nw
            o_ref[rows[i][0], rows[i][1], h * HEAD_DIM:(h + 1) * HEAD_DIM] = on

    def chunk_group(c, carry):
        slices = [pl.ds(pl.multiple_of((c * GDN_GROUP + j) * CHUNK, CHUNK), CHUNK) for j in range(GDN_GROUP)]
        rows = [(b, r) for r in slices for b in range(bsz)]
        parts = local(rows)
        for j in range(GDN_GROUP):
            recurrent(rows[j * bsz:(j + 1) * bsz], j * bsz, *parts)
        return carry

    lax.fori_loop(0, tb // (CHUNK * GDN_GROUP), chunk_group, 0)


def _gdn(hb, xg, layer, smalls, tb=1024):
    bsz, s, d = hb.shape
    tb = min(tb, s)
    return pl.pallas_call(
        _gdn_kernel,
        grid=(s // tb,),
        in_specs=[pl.BlockSpec((bsz, tb, d), lambda i: (0, i, 0)),
                  pl.BlockSpec((bsz, tb, 3 * HW), lambda i: (0, i, 0))] + [_layer_spec(a, layer) for a in smalls],
        out_specs=pl.BlockSpec((bsz, tb, HW), lambda i: (0, i, 0)),
        out_shape=jax.ShapeDtypeStruct((bsz, s, HW), F32),
        scratch_shapes=[pltpu.VMEM((bsz * HEADS, HEAD_DIM, HEAD_DIM), F32)],
        compiler_params=_params("arbitrary"),
        name="gated_deltanet",
    )(hb, xg, *smalls)


def _sb_kernel(q_ref, k_ref, v_ref, o_ref, out_scr, spent_scr):
    tq = q_ref.shape[1]
    tk = tq
    qi = pl.program_id(1)
    heads = range(HEADS)
    hs = lambda h: slice(h * HEAD_DIM, (h + 1) * HEAD_DIM)
    row = lax.broadcasted_iota(jnp.int32, (tq, tk), 0)
    col = lax.broadcasted_iota(jnp.int32, (tq, tk), 1)
    suffix = (row >= col).astype(BF16)
    suffix2 = jnp.concatenate([suffix, suffix], axis=0)
    q = [q_ref[0, :, hs(h)] for h in heads]
    wide = lambda a: jnp.concatenate([a] * (tk // HEAD_DIM), axis=1)

    def block(kj, spent, valid):
        keys = pl.ds(pl.multiple_of(kj * tk, tk), tk)
        z = [_dot_nt(q[h], k_ref[0, keys, hs(h)]) for h in heads]
        sp = [_softplus(z[h]) for h in heads]
        nlk = sp if valid is None else [jnp.where(valid, sp[h], 0.0) for h in heads]
        sums = [_dot(jnp.concatenate(_split(nlk[h]), axis=1), suffix2) for h in heads]
        wgt = [jnp.exp(z[h] - sums[h] - wide(spent[h])) for h in heads]
        if valid is not None:
            wgt = [jnp.where(valid, wgt[h], 0.0) for h in heads]
        pv = [_dot(wgt[h].astype(BF16), v_ref[0, keys, hs(h)]) for h in heads]
        spent = [spent[h] + jnp.broadcast_to(sums[h][:, 0:1], (tq, HEAD_DIM)) for h in heads]
        least = jnp.minimum(jnp.minimum(spent[0], spent[1]), jnp.minimum(spent[2], spent[3]))
        return pv, spent, jnp.min(least)

    zero = jnp.zeros((tq, HEAD_DIM), F32)
    pv, spent, least = block(qi, [zero] * HEADS, col < row)
    for h in heads:
        out_scr[:, hs(h)] = pv[h]
        spent_scr[h] = spent[h]

    def cond(carry):
        kj, spent_min = carry
        return jnp.logical_and(kj >= 0, spent_min < SB_STICK_EXHAUSTED)

    def body(carry):
        kj, _ = carry
        pv, spent, least = block(kj, [spent_scr[h] for h in heads], None)
        for h in heads:
            out_scr[:, hs(h)] += pv[h]
            spent_scr[h] = spent[h]
        return kj - 1, least

    lax.while_loop(cond, body, (qi - 1, least))
    o_ref[0] = out_scr[...].astype(o_ref.dtype)


def _stick_breaking(qkv, tq=256):
    bsz, s, _ = qkv.shape
    tq = min(tq, s)
    resident = lambda j: pl.BlockSpec((1, s, HW), lambda b, i: (b, 0, j), pipeline_mode=pl.Buffered(1))
    return pl.pallas_call(
        _sb_kernel,
        grid=(bsz, s // tq),
        in_specs=[pl.BlockSpec((1, tq, HW), lambda b, i: (b, i, 0)), resident(1), resident(2)],
        out_specs=pl.BlockSpec((1, tq, HW), lambda b, i: (b, i, 0)),
        out_shape=jax.ShapeDtypeStruct((bsz, s, HW), BF16),
        scratch_shapes=[pltpu.VMEM((tq, HW), F32), pltpu.VMEM((HEADS, tq, HEAD_DIM), F32)],
        compiler_params=_params("parallel", "arbitrary"),
        name="stick_breaking",
    )(qkv, qkv, qkv)


def _merge_ret_kernel(alpha, tiles_per_seq, x_ref, hb_ref, oa_ref, ob_ref, qk_ref, v_ref, cos_ref, sin_ref,
                      rnw_ref, wg_ref, wbr_ref, wo_ref, g_ref, b_ref, o_ref, obf_ref,
                      state_scr, oc_scr, part_scr, gate_scr):
    tm = x_ref.shape[0]
    d = wo_ref.shape[0]
    heads = range(HEADS)
    o_ma = HW + RV_W

    @pl.when(pl.program_id(0) % tiles_per_seq == 0)
    def _():
        state_scr[...] = jnp.zeros_like(state_scr)

    ii = lax.broadcasted_iota(jnp.int32, (CHUNK, CHUNK), 0)
    jj = lax.broadcasted_iota(jnp.int32, (CHUNK, CHUNK), 1)
    rel = (ii - jj).astype(F32)
    pos = ii.astype(F32)
    rnw = rnw_ref[...]
    log_gamma = [math.log(1.0 - 2.0 ** (-5.0 - h)) for h in heads]
    dmat = [jnp.where(rel >= 0, jnp.exp(jnp.maximum(rel, 0.0) * lg), 0.0) for lg in log_gamma]
    q_dec = [jnp.exp((pos + 1.0) * lg) for lg in log_gamma]
    k_dec = [jnp.exp((CHUNK - 1.0 - pos) * lg) * (HEAD_DIM ** -0.5) for lg in log_gamma]

    def rot(t, cos, sin):
        return t * cos + pltpu.roll(t, HEAD_DIM // 2, 1) * sin

    def silu(t):
        return t * _sigmoid(t)

    def ret_chunk(c):
        rows = slice(c * CHUNK, (c + 1) * CHUNK)
        cos = cos_ref[rows, :]
        sin = sin_ref[rows, :]
        q = [rot(qk_ref[rows, h * HEAD_DIM:(h + 1) * HEAD_DIM], cos, sin) for h in heads]
        k = [rot(qk_ref[rows, HW + h * HEAD_DIM:HW + (h + 1) * HEAD_DIM], cos, sin) for h in heads]
        v = [v_ref[rows, h * RET_DV:(h + 1) * RET_DV] for h in heads]
        a = [(_dot_nt(q[h].astype(BF16), (k[h] * (HEAD_DIM ** -0.5)).astype(BF16)) * dmat[h]).astype(BF16)
             for h in heads]
        state = [state_scr[h] for h in heads]
        o = [_dot(a[h], v[h]) + _dot((q[h] * q_dec[h]).astype(BF16), state[h].astype(BF16)) for h in heads]
        for h in heads:
            state_scr[h] = state[h] * math.exp(CHUNK * log_gamma[h]) + _dot_tn((k[h] * k_dec[h]).astype(BF16), v[h])
        for h in heads:
            mu = jnp.mean(o[h], axis=-1, keepdims=True)
            dev = o[h] - mu
            var = jnp.mean(dev * dev, axis=-1, keepdims=True)
            oc_scr[rows, h * RET_DV:(h + 1) * RET_DV] = dev * lax.rsqrt(var + NORM_EPS) * rnw

    hb = hb_ref[...]
    oa = (oa_ref[...] * silu(_dot(hb, wg_ref[:, 0:HW]))).astype(BF16)
    ob = ob_ref[...]
    gate = lambda j, lo: _sigmoid(_dot(hb, wg_ref[:, o_ma + j * d + lo:o_ma + j * d + lo + MERGE_CHUNK]))
    n_cols = d // MERGE_CHUNK
    n_ret = tm // CHUNK
    for c in range(max(n_cols, n_ret)):
        if c < n_cols:
            lo = c * MERGE_CHUNK
            cs = slice(lo, lo + MERGE_CHUNK)
            part_scr[:, cs] = (gate(0, lo) * _dot(oa, wbr_ref[0:HW, cs])
                               + gate(1, lo) * _dot(ob, wbr_ref[HW:2 * HW, cs]))
            gate_scr[:, cs] = gate(2, lo)
        if c < n_ret:
            ret_chunk(c)
    oc = (oc_scr[...] * silu(_dot(hb, wg_ref[:, HW:o_ma]))).astype(BF16)
    mix = None
    for c in range(n_cols):
        lo = c * MERGE_CHUNK
        cs = slice(lo, lo + MERGE_CHUNK)
        merged = part_scr[:, cs] + gate_scr[:, cs] * _dot(oc, wbr_ref[2 * HW:2 * HW + RV_W, cs])
        part = _dot(merged.astype(BF16), wo_ref[cs, :])
        mix = part if mix is None else mix + part
    y = _layer_norm(alpha * x_ref[...] + mix, g_ref[...], b_ref[...])
    o_ref[...] = y
    obf_ref[...] = y.astype(BF16)


def _merge_ret(x, hb, oa, ob, rqk, rv, cos, sin, layer, consts, alpha, s, tm=512):
    t, d = x.shape
    tm = min(tm, s)
    tiles_per_seq = s // tm
    acts = [x, hb, oa, ob, rqk, rv]
    table = pl.BlockSpec((tm, HEAD_DIM), lambda i: (i % tiles_per_seq, 0))
    return pl.pallas_call(
        functools.partial(_merge_ret_kernel, alpha, tiles_per_seq),
        grid=(t // tm,),
        in_specs=[pl.BlockSpec((tm, a.shape[1]), lambda i: (i, 0)) for a in acts] + [table, table]
        + [_layer_spec(w, layer) for w in consts],
        out_specs=[pl.BlockSpec((tm, d), lambda i: (i, 0)), pl.BlockSpec((tm, d), lambda i: (i, 0))],
        out_shape=[jax.ShapeDtypeStruct((t, d), F32), jax.ShapeDtypeStruct((t, d), BF16)],
        scratch_shapes=[pltpu.VMEM((HEADS, HEAD_DIM, RET_DV), F32), pltpu.VMEM((tm, RV_W), F32),
                        pltpu.VMEM((tm, d), F32), pltpu.VMEM((tm, d), F32)],
        compiler_params=_params("arbitrary"),
        name="retention_merge_ln",
    )(*acts, cos, sin, *consts)


REGROUP_ROWS = 512
SUBLANES = 8


def _regroup_kernel(scaled_block, wt_ref, o_ref):
    x = wt_ref[0]
    scale = jnp.where(pl.program_id(1) == scaled_block, HEAD_DIM ** -0.5, 1.0).astype(F32)
    o_ref[...] = (x * scale).T.astype(BF16)


def _regroup(w_in_t, runs, scaled_block=-1):
    depth, _, d = w_in_t.shape
    n_blocks = sum(n for _, n in runs)

    def src_row(j):
        tile, first = 0, 0
        for start, n in runs:
            tile = jnp.where(j >= first, start // SUBLANES + (j - first) * (REGROUP_ROWS // SUBLANES), tile)
            first += n
        return tile * SUBLANES

    return pl.pallas_call(
        functools.partial(_regroup_kernel, scaled_block),
        grid=(depth, n_blocks),
        in_specs=[pl.BlockSpec((pl.Element(1), pl.Element(REGROUP_ROWS), pl.Element(d)),
                               lambda l, j: (l, src_row(j), 0))],
        out_specs=pl.BlockSpec((None, d, REGROUP_ROWS), lambda l, j: (l, 0, j)),
        out_shape=jax.ShapeDtypeStruct((depth, d, n_blocks * REGROUP_ROWS), BF16),
        compiler_params=_params("parallel", "parallel"),
        name="regroup_w_in",
    )(w_in_t)


def _cast_rows_kernel(wt_ref, o_ref):
    o_ref[...] = wt_ref[0].astype(BF16)


def _cast_rows(w_in_t, start, n):
    depth, _, d = w_in_t.shape
    return pl.pallas_call(
        _cast_rows_kernel,
        grid=(depth,),
        in_specs=[pl.BlockSpec((pl.Element(1), pl.Element(n), pl.Element(d)), lambda l: (l, start, 0))],
        out_specs=pl.BlockSpec((None, n, d), lambda l: (l, 0, 0)),
        out_shape=jax.ShapeDtypeStruct((depth, n, d), BF16),
        compiler_params=_params("parallel"),
        name="cast_w_in_rows",
    )(w_in_t)


def _rotary_tables(s):
    half = HEAD_DIM // 2
    inv = ROPE_BASE ** (-jnp.arange(half, dtype=F32) / half)
    ang = jnp.arange(s).astype(F32)[:, None] * inv[None, :]
    cos, sin = jnp.cos(ang), jnp.sin(ang)
    return jnp.concatenate([cos, cos], axis=-1), jnp.concatenate([-sin, sin], axis=-1)


def kernel(x, p, ffn1_w13, ffn1_w2, w_in, gdn_conv_w, gdn_a_log, gdn_dt_bias, gdn_norm_w, ret_norm_w, w_br_gdn, w_br_sb, w_br_ret, w_out, ffn2_w13, ffn2_w2, ln_g, ln_b, w_ple_gate, w_ple_proj):
    bsz, s, d = x.shape
    depth = ffn1_w13.shape[0]
    t = bsz * s
    alpha = (2 * depth) ** 0.25
    bf = lambda a: a.astype(BF16)
    w_in_t = jnp.swapaxes(w_in, 1, 2)
    o_sq = 4 * HW + 2 * HEADS
    o_rg = o_sq + 5 * HW + RV_W
    nb = lambda width: width // REGROUP_ROWS
    w_proj = _regroup(w_in_t, [(0, nb(3 * HW)), (o_sq, nb(o_rg - o_sq))], scaled_block=nb(3 * HW))
    w_gate = _regroup(w_in_t, [(3 * HW, nb(HW)), (o_rg, nb(RV_W + 3 * d))])
    w_branch = bf(jnp.concatenate([w_br_gdn, w_br_sb, w_br_ret], axis=1))
    w_ba_t = _cast_rows(w_in_t, 4 * HW, 2 * HEADS)
    wb_t, wa_t = w_ba_t[:, :HEADS], w_ba_t[:, HEADS:]
    gdn_smalls = [jnp.swapaxes(wb_t, 1, 2), jnp.swapaxes(wa_t, 1, 2), wa_t,
                  gdn_a_log.reshape(depth, 1, HEADS), gdn_dt_bias.reshape(depth, 1, HEADS),
                  gdn_a_log.reshape(depth, HEADS, 1), gdn_dt_bias.reshape(depth, HEADS, 1),
                  gdn_norm_w.reshape(depth, 1, HEAD_DIM)]
    ret_nw = ret_norm_w.reshape(depth, 1, RET_DV)
    ffn1 = (bf(ffn1_w13), bf(ffn1_w2))
    ffn2 = (bf(ffn2_w13), bf(ffn2_w2))
    w_o, w_pg, w_pe = bf(w_out), bf(w_ple_gate), bf(w_ple_proj)
    ln = lambda a, j: a[:, j].reshape(depth, 1, d)
    p2 = p.reshape(depth, t, p.shape[-1])
    cos, sin = _rotary_tables(s)

    xf = x.reshape(t, d)
    for i in range(depth):
        xf, xb = _ffn(xf, i, *ffn1, ln(ln_g, 0), ln(ln_b, 0), alpha)
        xg, sqkv, rqk, rv = _proj(xb, i, w_proj, gdn_conv_w, s)
        oa = _gdn(xb.reshape(bsz, s, d), xg.reshape(bsz, s, 3 * HW), i, gdn_smalls)
        ob = _stick_breaking(sqkv.reshape(bsz, s, 3 * HW))
        xf, xb = _merge_ret(xf, xb, oa.reshape(t, HW), ob.reshape(t, HW), rqk, rv, cos, sin, i,
                            [ret_nw, w_gate, w_branch, w_o, ln(ln_g, 1), ln(ln_b, 1)], alpha, s)
        xf, xb = _ffn(xf, i, *ffn2, ln(ln_g, 2), ln(ln_b, 2), alpha, ple=(p2, w_pg, w_pe))
    return xf.reshape(bsz, s, d)
```

```python
import functools
import math

import jax
import jax.numpy as jnp
from jax import lax
from jax.experimental import pallas as pl
from jax.experimental.pallas import tpu as pltpu

F32 = jnp.float32
BF16 = jnp.bfloat16

NORM_EPS = 1e-5
HEADS = 4
HEAD_DIM = 128
RET_DV = 256
GDN_CONV = 4
ROPE_BASE = 10000.0
CHUNK = 128
CONV_HALO = 8
GDN_GROUP = 2
MERGE_CHUNK = 512
FFN_CHUNK = 256
PROJ_CHUNK = 256
SB_STICK_EXHAUSTED = 110.0
VMEM_LIMIT = 56 * 1024 * 1024

HW = HEADS * HEAD_DIM
RV_W = HEADS * RET_DV


def _dot(a, b):
    return jnp.dot(a, b, preferred_element_type=F32)


def _dot_nt(a, b):
    return lax.dot_general(a, b, (((1,), (1,)), ((), ())), preferred_element_type=F32)


def _dot_tn(a, b):
    return lax.dot_general(a, b, (((0,), (0,)), ((), ())), preferred_element_type=F32)


def _split(a):
    hi = a.astype(BF16)
    lo = (a - hi.astype(F32)).astype(BF16)
    return hi, lo


def _softplus(x):
    return jnp.maximum(x, 0.0) + jnp.log(1.0 + jnp.exp(-jnp.abs(x)))


def _sigmoid(x):
    return 1.0 / (1.0 + jnp.exp(-x))


def _layer_norm(y, g, b):
    mu = jnp.mean(y, axis=-1, keepdims=True)
    d = y - mu
    var = jnp.mean(d * d, axis=-1, keepdims=True)
    return d * lax.rsqrt(var + NORM_EPS) * g + b


def _params(*sem):
    return pltpu.CompilerParams(dimension_semantics=sem, vmem_limit_bytes=VMEM_LIMIT)


def _layer_spec(a, layer):
    zeros = (0,) * (a.ndim - 1)
    return pl.BlockSpec((None,) + a.shape[1:], lambda *_: (layer,) + zeros, pipeline_mode=pl.Buffered(1))


def _ffn_kernel(alpha, with_ple, x_ref, w13_ref, w2_ref, g_ref, b_ref, *rest):
    if with_ple:
        p_ref, wpg_ref, wpe_ref, o_ref, ob_ref = rest
    else:
        o_ref, ob_ref = rest
    f = w2_ref.shape[0]
    x = x_ref[...]
    xb = x.astype(BF16)
    hs = []
    for s in range(f // FFN_CHUNK):
        lo = s * FFN_CHUNK
        g = _dot(xb, w13_ref[:, lo:lo + FFN_CHUNK])
        u = _dot(xb, w13_ref[:, f + lo:f + lo + FFN_CHUNK])
        hs.append((g * _sigmoid(g) * u).astype(BF16))
    acc = _dot(jnp.concatenate(hs, axis=1), w2_ref[...])
    y = _layer_norm(alpha * x + 0.5 * acc, g_ref[...], b_ref[...])
    if with_ple:
        gate = _sigmoid(_dot(y.astype(BF16), wpg_ref[...]))
        y = y + gate * _dot(p_ref[...].astype(BF16), wpe_ref[...])
    o_ref[...] = y
    ob_ref[...] = y.astype(BF16)


def _ffn(x, layer, w13, w2, g, b, alpha, ple=None, tm=1024):
    t, d = x.shape
    tm = min(tm, t)
    consts = [w13, w2, g, b]
    in_specs = [pl.BlockSpec((tm, d), lambda i: (i, 0))] + [_layer_spec(a, layer) for a in consts]
    args = [x] + consts
    if ple is not None:
        p, wpg, wpe = ple
        in_specs += [pl.BlockSpec((None, tm, p.shape[2]), lambda i: (layer, i, 0)),
                     _layer_spec(wpg, layer), _layer_spec(wpe, layer)]
        args += [p, wpg, wpe]
    return pl.pallas_call(
        functools.partial(_ffn_kernel, alpha, ple is not None),
        grid=(t // tm,),
        in_specs=in_specs,
        out_specs=[pl.BlockSpec((tm, d), lambda i: (i, 0)),
                   pl.BlockSpec((tm, d), lambda i: (i, 0))],
        out_shape=[jax.ShapeDtypeStruct((t, d), F32), jax.ShapeDtypeStruct((t, d), BF16)],
        compiler_params=_params("parallel"),
        name="ffn_ln_ple" if ple is not None else "ffn_ln",
    )(*args)


def _proj_kernel(tiles_per_seq, hb_ref, w_ref, cw_ref, xg_ref, sb_ref, rqk_ref, rv_ref, xpad_scr):
    tm = hb_ref.shape[0]
    gw = xg_ref.shape[1]

    @pl.when(pl.program_id(0) % tiles_per_seq == 0)
    def _():
        xpad_scr[0:CONV_HALO, :] = jnp.zeros((CONV_HALO, xpad_scr.shape[1]), F32)

    hb = hb_ref[...]
    def conv_chunk(c):
        cs = slice(c * PROJ_CHUNK, (c + 1) * PROJ_CHUNK)
        xpad_scr[CONV_HALO:CONV_HALO + tm, cs] = _dot(hb, w_ref[:, cs])
        acc = cw_ref[GDN_CONV - 1:GDN_CONV, cs] * xpad_scr[CONV_HALO:CONV_HALO + tm, cs]
        for k in range(GDN_CONV - 1):
            off = CONV_HALO - (GDN_CONV - 1) + k
            acc = acc + cw_ref[k:k + 1, cs] * xpad_scr[off:off + tm, cs]
        xpad_scr[0:CONV_HALO, cs] = xpad_scr[tm:tm + CONV_HALO, cs]
        xg_ref[:, cs] = acc * _sigmoid(acc)

    plain = []
    lo = gw
    for ref in (sb_ref, rqk_ref, rv_ref):
        for c in range(ref.shape[1] // PROJ_CHUNK):
            plain.append((ref, c * PROJ_CHUNK, lo + c * PROJ_CHUNK))
        lo += ref.shape[1]
    n_conv = gw // PROJ_CHUNK
    per = -(-len(plain) // n_conv)
    for c in range(n_conv):
        conv_chunk(c)
        for ref, dst, src in plain[c * per:(c + 1) * per]:
            ref[:, dst:dst + PROJ_CHUNK] = _dot(hb, w_ref[:, src:src + PROJ_CHUNK]).astype(ref.dtype)


def _proj(hb, layer, w, conv_w, s, tm=512):
    t, d = hb.shape
    tm = min(tm, s)
    widths = [3 * HW, 3 * HW, 2 * HW, RV_W]
    dts = [F32, BF16, F32, BF16]
    return pl.pallas_call(
        functools.partial(_proj_kernel, s // tm),
        grid=(t // tm,),
        in_specs=[pl.BlockSpec((tm, d), lambda i: (i, 0)), _layer_spec(w, layer), _layer_spec(conv_w, layer)],
        out_specs=[pl.BlockSpec((tm, n), lambda i: (i, 0)) for n in widths],
        out_shape=[jax.ShapeDtypeStruct((t, n), dt) for n, dt in zip(widths, dts)],
        scratch_shapes=[pltpu.VMEM((tm + CONV_HALO, widths[0]), F32)],
        compiler_params=_params("arbitrary"),
        name="mixer_in_proj",
    )(hb, w, conv_w)


def _unit_lower_inverse(lows, ii, jj):
    eye = (ii == jj).astype(F32)
    diag8 = (ii >> 3) == (jj >> 3)
    xs = [jnp.where(diag8, -low, 0.0) for low in lows]
    xbs = [x.astype(BF16) for x in xs]
    x2s = [_dot(xb, xb).astype(BF16) for xb in xbs]
    ts = [eye + x for x in xs]
    ts = [t + _dot(t.astype(BF16), x2) for t, x2 in zip(ts, x2s)]
    x4s = [_dot(x2, x2).astype(BF16) for x2 in x2s]
    ts = [t + _dot(t.astype(BF16), x4) for t, x4 in zip(ts, x4s)]
    shift = 3
    while (1 << shift) < CHUNK:
        pair = ((ii >> (shift + 1)) == (jj >> (shift + 1))) & ((ii >> shift) != (jj >> shift))
        tbs = [t.astype(BF16) for t in ts]
        mids = [_dot(jnp.where(pair, low, 0.0).astype(BF16), tb).astype(BF16) for low, tb in zip(lows, tbs)]
        ts = [t - _dot(tb, mid) for t, tb, mid in zip(ts, tbs, mids)]
        shift += 1
    return ts


def _gdn_kernel(hb_ref, qkv_ref, wb_ref, wa_ref, wat_ref, alog_r_ref, dtb_r_ref,
                alog_c_ref, dtb_c_ref, nw_ref, o_ref, state_scr):
    bsz, tb, _ = qkv_ref.shape

    @pl.when(pl.program_id(0) == 0)
    def _():
        state_scr[...] = jnp.zeros_like(state_scr)

    ii = lax.broadcasted_iota(jnp.int32, (CHUNK, CHUNK), 0)
    jj = lax.broadcasted_iota(jnp.int32, (CHUNK, CHUNK), 1)
    tri_incl = (ii >= jj).astype(BF16)
    tri_incl_t = (ii <= jj).astype(BF16)
    neg_a_r = -jnp.exp(alog_r_ref[...])
    neg_a_c = -jnp.exp(alog_c_ref[...])
    nw = nw_ref[...]
    sq = (CHUNK, CHUNK)
    heads = range(HEADS)

    def l2n(t):
        return t * lax.rsqrt(jnp.sum(t * t, axis=-1, keepdims=True) + 1e-6)

    def local(rows):
        pairs = [(r, h) for r in rows for h in heads]
        cols, cum_rows, s_decs = [], [], []
        for b, r in rows:
            hb_c = hb_ref[b, r, :]
            beta_col = _sigmoid(_dot(hb_c, wb_ref[...]))
            g_col = neg_a_r * _softplus(_dot(hb_c, wa_ref[...]) + dtb_r_ref[...])
            g_row = neg_a_c * _softplus(_dot_nt(wat_ref[...], hb_c) + dtb_c_ref[...])
            gch, gcl = _split(g_col)
            grh, grl = _split(g_row)
            cum_col = _dot(tri_incl, gch) + _dot(tri_incl, gcl)
            cum_rows.append(_dot(grh, tri_incl_t) + _dot(grl, tri_incl_t))
            cum_last = cum_col[CHUNK - 1:CHUNK, :]
            cols.append((beta_col, cum_col, jnp.exp(cum_col), jnp.exp(cum_last - cum_col)))
            s_decs.append(jnp.exp(cum_last))

        n = len(pairs)
        col = lambda j, which, h: jnp.broadcast_to(cols[j][which][:, h:h + 1], sq)
        chunk_of = [j for j in range(len(rows)) for _ in heads]
        head_of = [h for _ in rows for h in heads]
        q = [l2n(qkv_ref[b, r, h * HEAD_DIM:(h + 1) * HEAD_DIM]) * (HEAD_DIM ** -0.5) for (b, r), h in pairs]
        k = [l2n(qkv_ref[b, r, HW + h * HEAD_DIM:HW + (h + 1) * HEAD_DIM]) for (b, r), h in pairs]
        v = [qkv_ref[b, r, 2 * HW + h * HEAD_DIM:2 * HW + (h + 1) * HEAD_DIM] for (b, r), h in pairs]
        beta = [col(chunk_of[p], 0, head_of[p]) for p in range(n)]
        e_g = [col(chunk_of[p], 2, head_of[p]) for p in range(n)]
        dec = [jnp.exp(jnp.where(ii >= jj, col(chunk_of[p], 1, head_of[p])
                                 - jnp.broadcast_to(cum_rows[chunk_of[p]][head_of[p]:head_of[p] + 1, :], sq), -1e30))
               for p in range(n)]
        kb = [k[p] * beta[p] for p in range(n)]
        kbf = [k[p].astype(BF16) for p in range(n)]
        kq = [_dot_nt(jnp.concatenate([kb[p], q[p]], axis=0).astype(BF16), kbf[p]) for p in range(n)]
        low = [jnp.where(ii > jj, kq[p][:CHUNK] * dec[p], 0.0) for p in range(n)]
        t_inv = _unit_lower_inverse(low, ii, jj)
        rhs = [jnp.concatenate([v[p] * beta[p], kb[p] * e_g[p]], axis=1).astype(BF16) for p in range(n)]
        sol = [_dot(t_inv[p].astype(BF16), rhs[p]) for p in range(n)]
        a_qk = [(kq[p][CHUNK:] * dec[p]).astype(BF16) for p in range(n)]
        wq = [jnp.concatenate([sol[p][:, HEAD_DIM:], q[p] * e_g[p]], axis=0).astype(BF16) for p in range(n)]
        u = [sol[p][:, :HEAD_DIM] for p in range(n)]
        k_dec = [(k[p] * col(chunk_of[p], 3, head_of[p])).astype(BF16) for p in range(n)]
        return u, wq, a_qk, k_dec, s_decs

    def recurrent(rows, c0, u, wq, a_qk, k_dec, s_decs):
        chains = [(i, h) for i in range(len(rows)) for h in heads]
        state = [state_scr[rows[i][0] * HEADS + h] for i, h in chains]
        pair = [(c0 + i) * HEADS + h for i, h in chains]
        from_state = [_dot(wq[p], s.astype(BF16)) for p, s in zip(pair, state)]
        v_new = [(u[p] - fs[:CHUNK]).astype(BF16) for p, fs in zip(pair, from_state)]
        o = [fs[CHUNK:] + _dot(a_qk[p], vn) for p, fs, vn in zip(pair, from_state, v_new)]
        for (i, h), p, s, vn in zip(chains, pair, state, v_new):
            state_scr[rows[i][0] * HEADS + h] = s * s_decs[c0 + i][:, h:h + 1] + _dot_tn(k_dec[p], vn)
        for (i, h), on in zip(chains, o):
            on = on * lax.rsqrt(jnp.mean(on * on, axis=-1, keepdims=True) + NORM_EPS) * nw
            o_ref[rows[i][0], rows[i][1], h * HEAD_DIM:(h + 1) * HEAD_DIM] = on

    def chunk_group(c, carry):
        slices = [pl.ds(pl.multiple_of((c * GDN_GROUP + j) * CHUNK, CHUNK), CHUNK) for j in range(GDN_GROUP)]
        rows = [(b, r) for r in slices for b in range(bsz)]
        parts = local(rows)
        for j in range(GDN_GROUP):
            recurrent(rows[j * bsz:(j + 1) * bsz], j * bsz, *parts)
        return carry

    lax.fori_loop(0, tb // (CHUNK * GDN_GROUP), chunk_group, 0)


def _gdn(hb, xg, layer, smalls, tb=1024):
    bsz, s, d = hb.shape
    tb = min(tb, s)
    return pl.pallas_call(
        _gdn_kernel,
        grid=(s // tb,),
        in_specs=[pl.BlockSpec((bsz, tb, d), lambda i: (0, i, 0)),
                  pl.BlockSpec((bsz, tb, 3 * HW), lambda i: (0, i, 0))] + [_layer_spec(a, layer) for a in smalls],
        out_specs=pl.BlockSpec((bsz, tb, HW), lambda i: (0, i, 0)),
        out_shape=jax.ShapeDtypeStruct((bsz, s, HW), F32),
        scratch_shapes=[pltpu.VMEM((bsz * HEADS, HEAD_DIM, HEAD_DIM), F32)],
        compiler_params=_params("arbitrary"),
        name="gated_deltanet",
    )(hb, xg, *smalls)


def _sb_kernel(q_ref, k_ref, v_ref, o_ref, out_scr, spent_scr):
    tq = q_ref.shape[1]
    tk = tq
    qi = pl.program_id(1)
    heads = range(HEADS)
    hs = lambda h: slice(h * HEAD_DIM, (h + 1) * HEAD_DIM)
    row = lax.broadcasted_iota(jnp.int32, (tq, tk), 0)
    col = lax.broadcasted_iota(jnp.int32, (tq, tk), 1)
    suffix = (row >= col).astype(BF16)
    suffix2 = jnp.concatenate([suffix, suffix], axis=0)
    q = [q_ref[0, :, hs(h)] for h in heads]
    wide = lambda a: jnp.concatenate([a] * (tk // HEAD_DIM), axis=1)

    def block(kj, spent, valid):
        keys = pl.ds(pl.multiple_of(kj * tk, tk), tk)
        z = [_dot_nt(q[h], k_ref[0, keys, hs(h)]) for h in heads]
        sp = [_softplus(z[h]) for h in heads]
        nlk = sp if valid is None else [jnp.where(valid, sp[h], 0.0) for h in heads]
        sums = [_dot(jnp.concatenate(_split(nlk[h]), axis=1), suffix2) for h in heads]
        wgt = [jnp.exp(z[h] - sums[h] - wide(spent[h])) for h in heads]
        if valid is not None:
            wgt = [jnp.where(valid, wgt[h], 0.0) for h in heads]
        pv = [_dot(wgt[h].astype(BF16), v_ref[0, keys, hs(h)]) for h in heads]
        spent = [spent[h] + jnp.broadcast_to(sums[h][:, 0:1], (tq, HEAD_DIM)) for h in heads]
        least = jnp.minimum(jnp.minimum(spent[0], spent[1]), jnp.minimum(spent[2], spent[3]))
        return pv, spent, jnp.min(least)

    zero = jnp.zeros((tq, HEAD_DIM), F32)
    pv, spent, least = block(qi, [zero] * HEADS, col < row)
    for h in heads:
        out_scr[:, hs(h)] = pv[h]
        spent_scr[h] = spent[h]

    def cond(carry):
        kj, spent_min = carry
        return jnp.logical_and(kj >= 0, spent_min < SB_STICK_EXHAUSTED)

    def body(carry):
        kj, _ = carry
        pv, spent, least = block(kj, [spent_scr[h] for h in heads], None)
        for h in heads:
            out_scr[:, hs(h)] += pv[h]
            spent_scr[h] = spent[h]
        return kj - 1, least

    lax.while_loop(cond, body, (qi - 1, least))
    o_ref[0] = out_scr[...].astype(o_ref.dtype)


def _stick_breaking(qkv, tq=256):
    bsz, s, _ = qkv.shape
    tq = min(tq, s)
    resident = lambda j: pl.BlockSpec((1, s, HW), lambda b, i: (b, 0, j), pipeline_mode=pl.Buffered(1))
    return pl.pallas_call(
        _sb_kernel,
        grid=(bsz, s // tq),
        in_specs=[pl.BlockSpec((1, tq, HW), lambda b, i: (b, i, 0)), resident(1), resident(2)],
        out_specs=pl.BlockSpec((1, tq, HW), lambda b, i: (b, i, 0)),
        out_shape=jax.ShapeDtypeStruct((bsz, s, HW), BF16),
        scratch_shapes=[pltpu.VMEM((tq, HW), F32), pltpu.VMEM((HEADS, tq, HEAD_DIM), F32)],
        compiler_params=_params("parallel", "arbitrary"),
        name="stick_breaking",
    )(qkv, qkv, qkv)


def _merge_ret_kernel(alpha, tiles_per_seq, x_ref, hb_ref, oa_ref, ob_ref, qk_ref, v_ref, cos_ref, sin_ref,
                      rnw_ref, wg_ref, wbr_ref, wo_ref, g_ref, b_ref, o_ref, obf_ref,
                      state_scr, oc_scr, part_scr, gate_scr):
    tm = x_ref.shape[0]
    d = wo_ref.shape[0]
    heads = range(HEADS)
    o_ma = HW + RV_W

    @pl.when(pl.program_id(0) % tiles_per_seq == 0)
    def _():
        state_scr[...] = jnp.zeros_like(state_scr)

    ii = lax.broadcasted_iota(jnp.int32, (CHUNK, CHUNK), 0)
    jj = lax.broadcasted_iota(jnp.int32, (CHUNK, CHUNK), 1)
    rel = (ii - jj).astype(F32)
    pos = ii.astype(F32)
    rnw = rnw_ref[...]
    log_gamma = [math.log(1.0 - 2.0 ** (-5.0 - h)) for h in heads]
    dmat = [jnp.where(rel >= 0, jnp.exp(jnp.maximum(rel, 0.0) * lg), 0.0) for lg in log_gamma]
    q_dec = [jnp.exp((pos + 1.0) * lg) for lg in log_gamma]
    k_dec = [jnp.exp((CHUNK - 1.0 - pos) * lg) * (HEAD_DIM ** -0.5) for lg in log_gamma]

    def rot(t, cos, sin):
        return t * cos + pltpu.roll(t, HEAD_DIM // 2, 1) * sin

    def silu(t):
        return t * _sigmoid(t)

    def ret_chunk(c):
        rows = slice(c * CHUNK, (c + 1) * CHUNK)
        cos = cos_ref[rows, :]
        sin = sin_ref[rows, :]
        q = [rot(qk_ref[rows, h * HEAD_DIM:(h + 1) * HEAD_DIM], cos, sin) for h in heads]
        k = [rot(qk_ref[rows, HW + h * HEAD_DIM:HW + (h + 1) * HEAD_DIM], cos, sin) for h in heads]
        v = [v_ref[rows, h * RET_DV:(h + 1) * RET_DV] for h in heads]
        a = [(_dot_nt(q[h].astype(BF16), (k[h] * (HEAD_DIM ** -0.5)).astype(BF16)) * dmat[h]).astype(BF16)
             for h in heads]
        state = [state_scr[h] for h in heads]
        o = [_dot(a[h], v[h]) + _dot((q[h] * q_dec[h]).astype(BF16), state[h].astype(BF16)) for h in heads]
        for h in heads:
            state_scr[h] = state[h] * math.exp(CHUNK * log_gamma[h]) + _dot_tn((k[h] * k_dec[h]).astype(BF16), v[h])
        for h in heads:
            mu = jnp.mean(o[h], axis=-1, keepdims=True)
            dev = o[h] - mu
            var = jnp.mean(dev * dev, axis=-1, keepdims=True)
            oc_scr[rows, h * RET_DV:(h + 1) * RET_DV] = dev * lax.rsqrt(var + NORM_EPS) * rnw

    hb = hb_ref[...]
    oa = (oa_ref[...] * silu(_dot(hb, wg_ref[:, 0:HW]))).astype(BF16)
    ob = ob_ref[...]
    gate = lambda j, lo: _sigmoid(_dot(hb, wg_ref[:, o_ma + j * d + lo:o_ma + j * d + lo + MERGE_CHUNK]))
    n_cols = d // MERGE_CHUNK
    n_ret = tm // CHUNK
    for c in range(max(n_cols, n_ret)):
        if c < n_cols:
            lo = c * MERGE_CHUNK
            cs = slice(lo, lo + MERGE_CHUNK)
            part_scr[:, cs] = (gate(0, lo) * _dot(oa, wbr_ref[0:HW, cs])
                               + gate(1, lo) * _dot(ob, wbr_ref[HW:2 * HW, cs]))
            gate_scr[:, cs] = gate(2, lo)
        if c < n_ret:
            ret_chunk(c)
    oc = (oc_scr[...] * silu(_dot(hb, wg_ref[:, HW:o_ma]))).astype(BF16)
    mix = None
    for c in range(n_cols):
        lo = c * MERGE_CHUNK
        cs = slice(lo, lo + MERGE_CHUNK)
        merged = part_scr[:, cs] + gate_scr[:, cs] * _dot(oc, wbr_ref[2 * HW:2 * HW + RV_W, cs])
        part = _dot(merged.astype(BF16), wo_ref[cs, :])
        mix = part if mix is None else mix + part
    y = _layer_norm(alpha * x_ref[...] + mix, g_ref[...], b_ref[...])
    o_ref[...] = y
    obf_ref[...] = y.astype(BF16)


def _merge_ret(x, hb, oa, ob, rqk, rv, cos, sin, layer, consts, alpha, s, tm=512):
    t, d = x.shape
    tm = min(tm, s)
    tiles_per_seq = s // tm
    acts = [x, hb, oa, ob, rqk, rv]
    table = pl.BlockSpec((tm, HEAD_DIM), lambda i: (i % tiles_per_seq, 0))
    return pl.pallas_call(
        functools.partial(_merge_ret_kernel, alpha, tiles_per_seq),
        grid=(t // tm,),
        in_specs=[pl.BlockSpec((tm, a.shape[1]), lambda i: (i, 0)) for a in acts] + [table, table]
        + [_layer_spec(w, layer) for w in consts],
        out_specs=[pl.BlockSpec((tm, d), lambda i: (i, 0)), pl.BlockSpec((tm, d), lambda i: (i, 0))],
        out_shape=[jax.ShapeDtypeStruct((t, d), F32), jax.ShapeDtypeStruct((t, d), BF16)],
        scratch_shapes=[pltpu.VMEM((HEADS, HEAD_DIM, RET_DV), F32), pltpu.VMEM((tm, RV_W), F32),
                        pltpu.VMEM((tm, d), F32), pltpu.VMEM((tm, d), F32)],
        compiler_params=_params("arbitrary"),
        name="retention_merge_ln",
    )(*acts, cos, sin, *consts)


REGROUP_ROWS = 512
SUBLANES = 8


def _regroup_kernel(scaled_block, wt_ref, o_ref):
    x = wt_ref[0]
    scale = jnp.where(pl.program_id(1) == scaled_block, HEAD_DIM ** -0.5, 1.0).astype(F32)
    o_ref[...] = (x * scale).T.astype(BF16)


def _regroup(w_in_t, runs, scaled_block=-1):
    depth, _, d = w_in_t.shape
    n_blocks = sum(n for _, n in runs)

    def src_row(j):
        tile, first = 0, 0
        for start, n in runs:
            tile = jnp.where(j >= first, start // SUBLANES + (j - first) * (REGROUP_ROWS // SUBLANES), tile)
            first += n
        return tile * SUBLANES

    return pl.pallas_call(
        functools.partial(_regroup_kernel, scaled_block),
        grid=(depth, n_blocks),
        in_specs=[pl.BlockSpec((pl.Element(1), pl.Element(REGROUP_ROWS), pl.Element(d)),
                               lambda l, j: (l, src_row(j), 0))],
        out_specs=pl.BlockSpec((None, d, REGROUP_ROWS), lambda l, j: (l, 0, j)),
        out_shape=jax.ShapeDtypeStruct((depth, d, n_blocks * REGROUP_ROWS), BF16),
        compiler_params=_params("parallel", "parallel"),
        name="regroup_w_in",
    )(w_in_t)


def _cast_rows_kernel(wt_ref, o_ref):
    o_ref[...] = wt_ref[0].astype(BF16)


def _cast_rows(w_in_t, start, n):
    depth, _, d = w_in_t.shape
    return pl.pallas_call(
        _cast_rows_kernel,
        grid=(depth,),
        in_specs=[pl.BlockSpec((pl.Element(1), pl.Element(n), pl.Element(d)), lambda l: (l, start, 0))],
        out_specs=pl.BlockSpec((None, n, d), lambda l: (l, 0, 0)),
        out_shape=jax.ShapeDtypeStruct((depth, n, d), BF16),
        compiler_params=_params("parallel"),
        name="cast_w_in_rows",
    )(w_in_t)


def _rotary_tables(s):
    half = HEAD_DIM // 2
    inv = ROPE_BASE ** (-jnp.arange(half, dtype=F32) / half)
    ang = jnp.arange(s).astype(F32)[:, None] * inv[None, :]
    cos, sin = jnp.cos(ang), jnp.sin(ang)
    return jnp.concatenate([cos, cos], axis=-1), jnp.concatenate([-sin, sin], axis=-1)


def kernel(x, p, ffn1_w13, ffn1_w2, w_in, gdn_conv_w, gdn_a_log, gdn_dt_bias, gdn_norm_w, ret_norm_w, w_br_gdn, w_br_sb, w_br_ret, w_out, ffn2_w13, ffn2_w2, ln_g, ln_b, w_ple_gate, w_ple_proj):
    bsz, s, d = x.shape
    depth = ffn1_w13.shape[0]
    t = bsz * s
    alpha = (2 * depth) ** 0.25
    bf = lambda a: a.astype(BF16)
    w_in_t = jnp.swapaxes(w_in, 1, 2)
    o_sq = 4 * HW + 2 * HEADS
    o_rg = o_sq + 5 * HW + RV_W
    nb = lambda width: width // REGROUP_ROWS
    w_proj = _regroup(w_in_t, [(0, nb(3 * HW)), (o_sq, nb(o_rg - o_sq))], scaled_block=nb(3 * HW))
    w_gate = _regroup(w_in_t, [(3 * HW, nb(HW)), (o_rg, nb(RV_W + 3 * d))])
    w_branch = bf(jnp.concatenate([w_br_gdn, w_br_sb, w_br_ret], axis=1))
    w_ba_t = _cast_rows(w_in_t, 4 * HW, 2 * HEADS)
    wb_t, wa_t = w_ba_t[:, :HEADS], w_ba_t[:, HEADS:]
    gdn_smalls = [jnp.swapaxes(wb_t, 1, 2), jnp.swapaxes(wa_t, 1, 2), wa_t,
                  gdn_a_log.reshape(depth, 1, HEADS), gdn_dt_bias.reshape(depth, 1, HEADS),
                  gdn_a_log.reshape(depth, HEADS, 1), gdn_dt_bias.reshape(depth, HEADS, 1),
                  gdn_norm_w.reshape(depth, 1, HEAD_DIM)]
    ret_nw = ret_norm_w.reshape(depth, 1, RET_DV)
    ffn1 = (bf(ffn1_w13), bf(ffn1_w2))
    ffn2 = (bf(ffn2_w13), bf(ffn2_w2))
    w_o, w_pg, w_pe = bf(w_out), bf(w_ple_gate), bf(w_ple_proj)
    ln = lambda a, j: a[:, j].reshape(depth, 1, d)
    p2 = p.reshape(depth, t, p.shape[-1])
    cos, sin = _rotary_tables(s)

    xf = x.reshape(t, d)
    for i in range(depth):
        xf, xb = _ffn(xf, i, *ffn1, ln(ln_g, 0), ln(ln_b, 0), alpha)
        xg, sqkv, rqk, rv = _proj(xb, i, w_proj, gdn_conv_w, s)
        oa = _gdn(xb.reshape(bsz, s, d), xg.reshape(bsz, s, 3 * HW), i, gdn_smalls)
        ob = _stick_breaking(sqkv.reshape(bsz, s, 3 * HW))
        xf, xb = _merge_ret(xf, xb, oa.reshape(t, HW), ob.reshape(t, HW), rqk, rv, cos, sin, i,
                            [ret_nw, w_gate, w_branch, w_o, ln(ln_g, 1), ln(ln_b, 1)], alpha, s)
        xf, xb = _ffn(xf, i, *ffn2, ln(ln_g, 2), ln(ln_b, 2), alpha, ple=(p2, w_pg, w_pe))
    return xf.reshape(bsz, s, d)
```
